```python
import math
import jax, jax.numpy as jnp
from jax import lax
import numpy as np

D_MODEL = 1024
BATCH = 4
SEQ = 8192
DEPTH = 2

N_HEADS = 8
HEAD_DIM = 64
V_DIM = 2 * HEAD_DIM
QK_W = N_HEADS * 2 * HEAD_DIM
ATTN_W = N_HEADS * V_DIM
Q_BLOCK = 128
CONV_W = D_MODEL
CONV_K = 31
POOL_W = D_MODEL
POOL_WINDOWS = (2, 4, 8, 16)
POOL_GROUPS = len(POOL_WINDOWS)
POOL_GW = POOL_W // POOL_GROUPS
N_BRANCH = 3
IN_COLS = 2 * CONV_W + 2 * QK_W + ATTN_W + POOL_W + N_BRANCH * D_MODEL
SPLITS = tuple(np.cumsum([2 * CONV_W, QK_W, QK_W, ATTN_W, POOL_W]).tolist())
FFN_W = 2816
FFN_K = 3
N_BUCKETS = 32
MAX_DIST = 128
EPS = 1e-6

kernel_name = "hybrid_gated_conformer_diffattn_pool_block"


def rmsnorm(x, g):
    xf = x.astype(jnp.float32)
    y = xf * lax.rsqrt(jnp.mean(xf * xf, axis=-1, keepdims=True) + EPS)
    return (y * g.astype(jnp.float32)).astype(x.dtype)


def layernorm(x, g, b):
    xf = x.astype(jnp.float32)
    mu = jnp.mean(xf, axis=-1, keepdims=True)
    var = jnp.mean(jnp.square(xf - mu), axis=-1, keepdims=True)
    y = (xf - mu) * lax.rsqrt(var + EPS)
    return (y * g.astype(jnp.float32) + b.astype(jnp.float32)).astype(x.dtype)


def causal_dwconv(h, w, b=None):
    K, C = w.shape
    y = lax.conv_general_dilated(h, w[:, None, :].astype(h.dtype), window_strides=(1,),
                                 padding=[(K - 1, 0)], dimension_numbers=('NWC', 'WIO', 'NWC'),
                                 feature_group_count=C)
    return y if b is None else y + b.astype(h.dtype)


def t5_bucket(rel):
    n = jnp.maximum(rel, 0)
    max_exact = N_BUCKETS // 2
    nf = jnp.maximum(n, 1).astype(jnp.float32)
    large = max_exact + (jnp.log(nf / max_exact) / math.log(MAX_DIST / max_exact)
                         * (N_BUCKETS - max_exact)).astype(jnp.int32)
    large = jnp.minimum(large, N_BUCKETS - 1)
    return jnp.where(n < max_exact, n, large)


def conformer_conv(u, dw_w, dw_b, ln_g, ln_b):
    a, g = jnp.split(u, 2, axis=-1)
    h = a * jax.nn.sigmoid(g)
    h = causal_dwconv(h, dw_w, dw_b)
    return jax.nn.silu(layernorm(h, ln_g, ln_b))


def diff_attention(q, k, v, lam, subln_g, lam_init, rel_bias):
    B, S = q.shape[0], q.shape[1]
    nb = S // Q_BLOCK
    scale = HEAD_DIM ** -0.5
    qb = q.reshape(B, nb, Q_BLOCK, N_HEADS, 2, HEAD_DIM).transpose(1, 0, 2, 3, 4, 5)
    kpos = jnp.arange(S)

    def block(args):
        i, qblk = args
        qpos = i * Q_BLOCK + jnp.arange(Q_BLOCK)
        rel = qpos[:, None] - kpos[None, :]
        bias = rel_bias[t5_bucket(rel)].astype(jnp.float32).transpose(2, 0, 1)
        s = jnp.einsum('bqhmd,bkhmd->bhmqk', qblk, k).astype(jnp.float32) * scale
        s = s + bias[None, :, None]
        s = jnp.where(rel >= 0, s, -jnp.inf)
        p = jax.nn.softmax(s, axis=-1)
        a = p[:, :, 0] - lam * p[:, :, 1]
        return jnp.einsum('bhqk,bkhd->bqhd', a.astype(v.dtype), v)

    o = lax.map(block, (jnp.arange(nb), qb))
    o = o.transpose(1, 0, 2, 3, 4).reshape(B, S, N_HEADS, V_DIM)
    o = rmsnorm(o, subln_g) * (1.0 - lam_init)
    return o.reshape(B, S, ATTN_W)


def multiscale_pool(p, w_grp, scale):
    B, S, _ = p.shape
    pg = p.reshape(B, S, POOL_GROUPS, POOL_GW).astype(jnp.float32)
    cs = jnp.concatenate([jnp.zeros((B, 1, POOL_GROUPS, POOL_GW), jnp.float32),
                          jnp.cumsum(pg, axis=1)], axis=1)
    t = jnp.arange(S)
    win = jnp.array(POOL_WINDOWS, jnp.int32)
    lo = jnp.maximum(t[:, None] + 1 - win[None, :], 0)
    cs_lo = jnp.take_along_axis(cs, lo[None, :, :, None], axis=1)
    cnt = (t[:, None] + 1 - lo).astype(jnp.float32)
    pooled = (cs[:, 1:] - cs_lo) / cnt[None, :, :, None]
    d = (pooled - pg).astype(p.dtype)
    y = jnp.einsum('bsgc,gcd->bsgd', d, w_grp).reshape(B, S, POOL_W)
    return y * scale


def conv_ffn(xn, w_up, w_dw, w_down):
    u = causal_dwconv(xn @ w_up, w_dw)
    a, g = jnp.split(u, 2, axis=-1)
    return (jax.nn.silu(g) * a) @ w_down


def setup_inputs(seed: int = 0) -> dict:
    key = jax.random.key(seed)
    ks = jax.random.split(key, 24)
    f32 = jnp.float32
    nrm = lambda k, shape, s: jax.random.normal(k, shape, f32) * s
    L, D = DEPTH, D_MODEL
    return {
        "x": nrm(ks[0], (BATCH, SEQ, D), 1.0),
        "rel_bias": nrm(ks[1], (N_BUCKETS, N_HEADS), 0.5),
        "norm1_g": 1.0 + nrm(ks[2], (L, D), 0.01),
        "w_in": nrm(ks[3], (L, D, IN_COLS), D ** -0.5),
        "conv_dw_w": nrm(ks[4], (L, CONV_K, CONV_W), CONV_K ** -0.5),
        "conv_dw_b": nrm(ks[5], (L, CONV_W), 0.02),
        "conv_ln_g": 1.0 + nrm(ks[6], (L, CONV_W), 0.01),
        "conv_ln_b": nrm(ks[7], (L, CONV_W), 0.02),
        "lam_q1": nrm(ks[8], (L, HEAD_DIM), 0.1),
        "lam_k1": nrm(ks[9], (L, HEAD_DIM), 0.1),
        "lam_q2": nrm(ks[10], (L, HEAD_DIM), 0.1),
        "lam_k2": nrm(ks[11], (L, HEAD_DIM), 0.1),
        "subln_g": 1.0 + nrm(ks[12], (L, V_DIM), 0.01),
        "pool_w": nrm(ks[13], (L, POOL_GROUPS, POOL_GW, POOL_GW), POOL_GW ** -0.5),
        "pool_scale": 1.0 + nrm(ks[14], (L, POOL_W), 0.1),
        "w_br_conv": nrm(ks[15], (L, CONV_W, D), CONV_W ** -0.5),
        "w_br_attn": nrm(ks[16], (L, ATTN_W, D), ATTN_W ** -0.5),
        "w_br_pool": nrm(ks[17], (L, POOL_W, D), POOL_W ** -0.5),
        "w_o": nrm(ks[18], (L, D, D), D ** -0.5),
        "norm2_g": 1.0 + nrm(ks[19], (L, D), 0.01),
        "ffn_up": nrm(ks[20], (L, D, 2 * FFN_W), D ** -0.5),
        "ffn_dw": nrm(ks[21], (L, FFN_K, 2 * FFN_W), FFN_K ** -0.5),
        "ffn_down": nrm(ks[22], (L, FFN_W, D), FFN_W ** -0.5),
        "final_g": 1.0 + nrm(ks[23], (D,), 0.01),
    }


def reference(x, rel_bias, norm1_g, w_in, conv_dw_w, conv_dw_b, conv_ln_g, conv_ln_b,
              lam_q1, lam_k1, lam_q2, lam_k2, subln_g, pool_w, pool_scale,
              w_br_conv, w_br_attn, w_br_pool, w_o, norm2_g, ffn_up, ffn_dw, ffn_down, final_g):
    B, S, D = x.shape
    for l in range(DEPTH):
        lam_init = 0.8 - 0.6 * math.exp(-0.3 * l)
        xn = rmsnorm(x, norm1_g[l])
        z = xn @ w_in[l]
        u_conv, q, k, v, u_pool, gate_logits = jnp.split(z, SPLITS, axis=-1)
        b_conv = conformer_conv(u_conv, conv_dw_w[l], conv_dw_b[l], conv_ln_g[l], conv_ln_b[l])
        lam = (jnp.exp(jnp.sum(lam_q1[l].astype(jnp.float32) * lam_k1[l].astype(jnp.float32)))
               - jnp.exp(jnp.sum(lam_q2[l].astype(jnp.float32) * lam_k2[l].astype(jnp.float32)))
               + lam_init)
        b_attn = diff_attention(q.reshape(B, S, N_HEADS, 2, HEAD_DIM),
                                k.reshape(B, S, N_HEADS, 2, HEAD_DIM),
                                v.reshape(B, S, N_HEADS, V_DIM),
                                lam, subln_g[l], lam_init, rel_bias)
        b_pool = multiscale_pool(u_pool, pool_w[l], pool_scale[l])
        g = jax.nn.sigmoid(gate_logits).reshape(B, S, N_BRANCH, D)
        merged = (g[:, :, 0] * (b_conv @ w_br_conv[l])
                  + g[:, :, 1] * (b_attn @ w_br_attn[l])
                  + g[:, :, 2] * (b_pool @ w_br_pool[l]))
        x = x + merged @ w_o[l]
        x = x + conv_ffn(rmsnorm(x, norm2_g[l]), ffn_up[l], ffn_dw[l], ffn_down[l])
    return rmsnorm(x, final_g)
```

```python
import functools
import math

import jax
import jax.numpy as jnp
import numpy as np
from jax import lax
from jax.experimental import pallas as pl
from jax.experimental.pallas import tpu as pltpu

F32 = jnp.float32
BF16 = jnp.bfloat16

D_MODEL = 1024
N_HEADS = 8
HEAD_DIM = 64
V_DIM = 2 * HEAD_DIM
CONV_K = 31
POOL_WINDOWS = (2, 4, 8, 16)
POOL_GW = D_MODEL // len(POOL_WINDOWS)
FFN_W = 2816
FFN_K = 3
N_BUCKETS = 32
MAX_DIST = 128
EPS = 1e-6

COL_CONV = 0
COL_Q = 2 * D_MODEL
COL_K = COL_Q + D_MODEL
COL_V = COL_K + D_MODEL
COL_POOL = COL_V + D_MODEL
COL_GATE = COL_POOL + D_MODEL
IN_COLS = COL_GATE + 3 * D_MODEL

LANES = 128
SUBLANES = 8
VMEM_LIMIT = 56 * 1024 * 1024
NEG_BIG = -1e30
LOG2E = 1.4426950408889634
CONV_HALO = 32
POOL_HALO = 16
FFN_CHUNK = 256


def _cparams(sem):
    return pltpu.CompilerParams(dimension_semantics=sem, vmem_limit_bytes=VMEM_LIMIT)


def _rms(x, g):
    ms = jnp.mean(x * x, axis=-1, keepdims=True)
    return x * lax.rsqrt(ms + EPS) * g


def _inproj_kernel(x_ref, g_ref, w_ref, cs_ref, o_ref, xn_ref):
    @pl.when(pl.program_id(1) == 0)
    def _():
        xn_ref[...] = _rms(x_ref[...], g_ref[...]).astype(BF16)

    acc = jnp.dot(xn_ref[...], w_ref[...], preferred_element_type=F32)
    o_ref[...] = (acc * cs_ref[...]).astype(BF16)


def _inproj(x2, g, w, colscale, tm, tn):
    n = x2.shape[0]
    return pl.pallas_call(
        _inproj_kernel,
        grid=(n // tm, IN_COLS // tn),
        in_specs=[
            pl.BlockSpec((tm, D_MODEL), lambda i, j: (i, 0)),
            pl.BlockSpec((1, D_MODEL), lambda i, j: (0, 0)),
            pl.BlockSpec((D_MODEL, tn), lambda i, j: (0, j)),
            pl.BlockSpec((1, tn), lambda i, j: (0, j)),
        ],
        out_specs=pl.BlockSpec((tm, tn), lambda i, j: (i, j)),
        out_shape=jax.ShapeDtypeStruct((n, IN_COLS), BF16),
        scratch_shapes=[pltpu.VMEM((tm, D_MODEL), BF16)],
        compiler_params=_cparams(("arbitrary", "arbitrary")),
        name="inproj",
    )(x2, g, w, colscale)


def _conv_kernel(a_ref, g_ref, ah_ref, gh_ref, w_ref, b_ref, lg_ref, lb_ref, o_ref,
                 h_ref, y_ref, *, tm, tiles_per_seq, rows):
    first = (pl.program_id(0) % tiles_per_seq) == 0
    n = tm + CONV_HALO
    main = a_ref[...].astype(F32) * jax.nn.sigmoid(g_ref[...].astype(F32))
    halo = ah_ref[...].astype(F32) * jax.nn.sigmoid(gh_ref[...].astype(F32))
    h = jnp.concatenate([jnp.where(first, 0.0, halo), main], axis=0)
    h_ref[0] = h
    for b in range(1, SUBLANES):
        h_ref[b] = pltpu.roll(h, n - b, 0)

    lead = CONV_HALO - (CONV_K - 1)

    def chunk(r, carry):
        r0 = pl.multiple_of(r * rows, rows)
        for cg in range(D_MODEL // LANES):
            cols = slice(cg * LANES, (cg + 1) * LANES)
            acc = jnp.broadcast_to(b_ref[:, cols], (rows, LANES))
            for j in range(CONV_K):
                off = lead + j
                start = pl.multiple_of(r0 + (off - off % SUBLANES), SUBLANES)
                acc = acc + w_ref[j:j + 1, cols] * h_ref[off % SUBLANES, pl.ds(start, rows), cols]
            y_ref[pl.ds(r0, rows), cols] = acc
        return carry

    lax.fori_loop(0, tm // rows, chunk, 0)

    y = y_ref[...]
    mu = jnp.mean(y, axis=-1, keepdims=True)
    yc = y - mu
    var = jnp.mean(yc * yc, axis=-1, keepdims=True)
    yn = yc * lax.rsqrt(var + EPS) * lg_ref[...] + lb_ref[...]
    o_ref[...] = (yn * jax.nn.sigmoid(yn)).astype(BF16)


def _conv_branch(z, w, b, lg, lb, seq, tm):
    n = z.shape[0]
    hb = tm // CONV_HALO
    halo_idx = lambda i: jnp.maximum(i * hb - 1, 0)
    kern = functools.partial(_conv_kernel, tm=tm, tiles_per_seq=seq // tm, rows=64)
    vec = pl.BlockSpec((1, D_MODEL), lambda i: (0, 0))
    return pl.pallas_call(
        kern,
        grid=(n // tm,),
        in_specs=[
            pl.BlockSpec((tm, D_MODEL), lambda i: (i, 0)),
            pl.BlockSpec((tm, D_MODEL), lambda i: (i, 1)),
            pl.BlockSpec((CONV_HALO, D_MODEL), lambda i: (halo_idx(i), 0)),
            pl.BlockSpec((CONV_HALO, D_MODEL), lambda i: (halo_idx(i), 1)),
            pl.BlockSpec((CONV_K, D_MODEL), lambda i: (0, 0)),
            vec, vec, vec,
        ],
        out_specs=pl.BlockSpec((tm, D_MODEL), lambda i: (i, 0)),
        out_shape=jax.ShapeDtypeStruct((n, D_MODEL), BF16),
        scratch_shapes=[pltpu.VMEM((SUBLANES, tm + CONV_HALO, D_MODEL), F32),
                        pltpu.VMEM((tm, D_MODEL), F32)],
        compiler_params=_cparams(("arbitrary",)),
        name="conv_branch",
    )(z, z, z, z, w, b, lg, lb)


def _attn_kernel(q_ref, k_ref, v_ref, bias_ref, lam_ref, sg_ref, o_ref,
                 m_ref, l_ref, acc_ref, *, t, lam_init):
    qi = pl.program_id(2)
    q = q_ref[...]
    lane = lax.broadcasted_iota(jnp.int32, q.shape, 1)
    zero = jnp.zeros_like(q)
    qmaps = (jnp.where(lane < HEAD_DIM, q, zero), jnp.where(lane >= HEAD_DIM, q, zero))

    m_ref[...] = jnp.full(m_ref.shape, NEG_BIG, F32)
    l_ref[...] = jnp.zeros(l_ref.shape, F32)
    acc_ref[...] = jnp.zeros(acc_ref.shape, F32)

    def block(kstart, bias):
        k = k_ref[pl.ds(kstart, t), :]
        v = v_ref[pl.ds(kstart, t), :]
        for mi in range(2):
            s = lax.dot_general(k, qmaps[mi], (((1,), (1,)), ((), ())),
                                preferred_element_type=F32)
            if bias is not None:
                s = s + bias
            m_old = m_ref[mi]
            m_new = jnp.maximum(m_old, jnp.max(s, axis=0, keepdims=True))
            alpha = jnp.exp2(m_old - m_new)
            p = jnp.exp2(s - m_new)
            l_ref[mi] = alpha * l_ref[mi] + jnp.sum(p, axis=0, keepdims=True)
            pv = lax.dot_general(v, p.astype(BF16), (((0,), (0,)), ((), ())),
                                 preferred_element_type=F32)
            acc_ref[mi] = alpha * acc_ref[mi] + pv
            m_ref[mi] = m_new

    block(pl.multiple_of(qi * t, t), bias_ref[0])

    @pl.when(qi >= 1)
    def _():
        block(pl.multiple_of((qi - 1) * t, t), bias_ref[1])

    def far(ki, carry):
        block(pl.multiple_of(ki * t, t), None)
        return carry

    lax.fori_loop(0, jnp.maximum(qi - 1, 0), far, 0)

    lam = (jnp.exp(jnp.sum(lam_ref[0:1, :] * lam_ref[1:2, :], axis=-1, keepdims=True))
           - jnp.exp(jnp.sum(lam_ref[2:3, :] * lam_ref[3:4, :], axis=-1, keepdims=True))
           + lam_init)
    o_t = acc_ref[0] / l_ref[0] - lam * (acc_ref[1] / l_ref[1])
    o = o_t.T
    o_ref[...] = (_rms(o, sg_ref[...]) * (1.0 - lam_init)).astype(BF16)


def _attention(z, bias, lam_vecs, subln_g, batch, seq, t, lam_init):
    n = z.shape[0]
    nq = seq // t
    kern = functools.partial(_attn_kernel, t=t, lam_init=lam_init)
    return pl.pallas_call(
        kern,
        grid=(batch, N_HEADS, nq),
        in_specs=[
            pl.BlockSpec((t, V_DIM), lambda b, h, i: (b * nq + i, COL_Q // V_DIM + h)),
            pl.BlockSpec((seq, V_DIM), lambda b, h, i: (b, COL_K // V_DIM + h)),
            pl.BlockSpec((seq, V_DIM), lambda b, h, i: (b, COL_V // V_DIM + h)),
            pl.BlockSpec((None, 2, t, t), lambda b, h, i: (h, 0, 0, 0)),
            pl.BlockSpec((4, HEAD_DIM), lambda b, h, i: (0, 0)),
            pl.BlockSpec((1, V_DIM), lambda b, h, i: (0, 0)),
        ],
        out_specs=pl.BlockSpec((t, V_DIM), lambda b, h, i: (b * nq + i, h)),
        out_shape=jax.ShapeDtypeStruct((n, D_MODEL), BF16),
        scratch_shapes=[pltpu.VMEM((2, 1, t), F32),
                        pltpu.VMEM((2, 1, t), F32),
                        pltpu.VMEM((2, V_DIM, t), F32)],
        compiler_params=_cparams(("arbitrary", "arbitrary", "arbitrary")),
        name="diff_attn",
    )(z, z, z, bias, lam_vecs, subln_g)


def _t5_buckets(n):
    rel = np.arange(n)
    max_exact = N_BUCKETS // 2
    nf = np.maximum(rel, 1).astype(np.float32)
    large = max_exact + (np.log(nf / np.float32(max_exact)) / np.float32(math.log(MAX_DIST / max_exact))
                         * np.float32(N_BUCKETS - max_exact)).astype(np.int32)
    large = np.minimum(large, N_BUCKETS - 1)
    return np.where(rel < max_exact, rel, large)


def _bias_tiles(rel_bias, seq, t):
    buckets = _t5_buckets(seq)
    assert np.all(buckets[t + 1:] == buckets[-1])
    jj, ii = np.meshgrid(np.arange(t), np.arange(t), indexing="ij")
    rel_d = ii - jj
    idx_d = buckets[np.maximum(rel_d, 0)]
    idx_s = buckets[rel_d + t]
    tbl = (rel_bias.astype(F32) - rel_bias[buckets[-1]].astype(F32)[None, :]) * LOG2E
    tbl = tbl.T
    diag = jnp.where(jnp.asarray(rel_d >= 0)[None], tbl[:, idx_d], NEG_BIG)
    sub = tbl[:, idx_s]
    return jnp.stack([diag, sub], axis=1)


def _pool_kernel(p_ref, ph_ref, w_ref, sc_ref, o_ref, buf_ref, *, tm, tiles_per_seq):
    pos = pl.program_id(0) % tiles_per_seq
    first = pos == 0
    buf_ref[POOL_HALO:POOL_HALO + tm, :] = p_ref[...].astype(F32)
    buf_ref[0:POOL_HALO, :] = jnp.where(first, 0.0, ph_ref[...].astype(F32))
    t_idx = pos * tm + lax.broadcasted_iota(jnp.int32, (tm, POOL_GW), 0)
    for g, win in enumerate(POOL_WINDOWS):
        cols = slice(g * POOL_GW, (g + 1) * POOL_GW)
        cur = buf_ref[POOL_HALO:POOL_HALO + tm, cols]
        tot = cur
        for back in range(1, win):
            tot = tot + buf_ref[POOL_HALO - back:POOL_HALO - back + tm, cols]
        cnt = jnp.minimum(t_idx + 1, win).astype(F32)
        d = (tot / cnt - cur).astype(BF16)
        y = jnp.dot(d, w_ref[g], preferred_element_type=F32)
        o_ref[:, cols] = (y * sc_ref[:, cols]).astype(BF16)


def _pool_branch(z, w, scale, seq, tm):
    n = z.shape[0]
    hb = tm // POOL_HALO
    pc = COL_POOL // D_MODEL
    kern = functools.partial(_pool_kernel, tm=tm, tiles_per_seq=seq // tm)
    return pl.pallas_call(
        kern,
        grid=(n // tm,),
        in_specs=[
            pl.BlockSpec((tm, D_MODEL), lambda i: (i, pc)),
            pl.BlockSpec((POOL_HALO, D_MODEL), lambda i: (jnp.maximum(i * hb - 1, 0), pc)),
            pl.BlockSpec((len(POOL_WINDOWS), POOL_GW, POOL_GW), lambda i: (0, 0, 0)),
            pl.BlockSpec((1, D_MODEL), lambda i: (0, 0)),
        ],
        out_specs=pl.BlockSpec((tm, D_MODEL), lambda i: (i, 0)),
        out_shape=jax.ShapeDtypeStruct((n, D_MODEL), BF16),
        scratch_shapes=[pltpu.VMEM((tm + POOL_HALO, D_MODEL), F32)],
        compiler_params=_cparams(("arbitrary",)),
        name="pool_branch",
    )(z, z, w, scale)


def _merge_kernel(bc_ref, ba_ref, bp_ref, gt_ref, x_ref, wc_ref, wa_ref, wp_ref, wo_ref, o_ref):
    def gate(b):
        return jax.nn.sigmoid(gt_ref[:, b * D_MODEL:(b + 1) * D_MODEL].astype(F32))

    m = gate(0) * jnp.dot(bc_ref[...], wc_ref[...], preferred_element_type=F32)
    m = m + gate(1) * jnp.dot(ba_ref[...], wa_ref[...], preferred_element_type=F32)
    m = m + gate(2) * jnp.dot(bp_ref[...], wp_ref[...], preferred_element_type=F32)
    o_ref[...] = x_ref[...] + jnp.dot(m.astype(BF16), wo_ref[...], preferred_element_type=F32)


def _merge(b_conv, b_attn, b_pool, z, x2, wc, wa, wp, wo, tm):
    n = x2.shape[0]
    tile = pl.BlockSpec((tm, D_MODEL), lambda i: (i, 0))
    wspec = pl.BlockSpec((D_MODEL, D_MODEL), lambda i: (0, 0))
    return pl.pallas_call(
        _merge_kernel,
        grid=(n // tm,),
        in_specs=[tile, tile, tile,
                  pl.BlockSpec((tm, 3 * D_MODEL), lambda i: (i, COL_GATE // (3 * D_MODEL))),
                  tile, wspec, wspec, wspec, wspec],
        out_specs=tile,
        out_shape=jax.ShapeDtypeStruct((n, D_MODEL), F32),
        compiler_params=_cparams(("arbitrary",)),
        name="merge",
    )(b_conv, b_attn, b_pool, z, x2, wc, wa, wp, wo)


def _ffn_kernel(h_ref, g_ref, wu_ref, dw_ref, wd_ref, fg_ref, o_ref,
                xn_ref, carry_ref, acc_ref, *, tm, tiles_per_seq, final_norm):
    first = (pl.program_id(0) % tiles_per_seq) == 0

    @pl.when(first)
    def _():
        carry_ref[...] = jnp.zeros(carry_ref.shape, F32)

    h = h_ref[...]
    xn_ref[...] = _rms(h, g_ref[...]).astype(BF16)
    acc_ref[...] = jnp.zeros(acc_ref.shape, F32)

    def conv3(cols):
        u = jnp.dot(xn_ref[...], wu_ref[:, cols], preferred_element_type=F32)
        ext = jnp.concatenate([carry_ref[:, cols], u], axis=0)
        carry_ref[:, cols] = u[tm - SUBLANES:, :]
        y = dw_ref[2:3, cols] * u
        for back in range(1, FFN_K):
            shifted = pltpu.roll(ext, back, 0)[SUBLANES:, :]
            y = y + dw_ref[FFN_K - 1 - back:FFN_K - back, cols] * shifted
        return y

    for c in range(FFN_W // FFN_CHUNK):
        a = conv3(slice(c * FFN_CHUNK, (c + 1) * FFN_CHUNK))
        g = conv3(slice(FFN_W + c * FFN_CHUNK, FFN_W + (c + 1) * FFN_CHUNK))
        mid = (g * jax.nn.sigmoid(g) * a).astype(BF16)
        acc_ref[...] += jnp.dot(mid, wd_ref[c * FFN_CHUNK:(c + 1) * FFN_CHUNK, :],
                                preferred_element_type=F32)

    out = h + acc_ref[...]
    if final_norm:
        out = _rms(out, fg_ref[...])
    o_ref[...] = out


def _ffn(h2, g, w_up, w_dw, w_down, final_g, seq, tm, final_norm):
    n = h2.shape[0]
    kern = functools.partial(_ffn_kernel, tm=tm, tiles_per_seq=seq // tm, final_norm=final_norm)
    tile = pl.BlockSpec((tm, D_MODEL), lambda i: (i, 0))
    vec = pl.BlockSpec((1, D_MODEL), lambda i: (0, 0))
    return pl.pallas_call(
        kern,
        grid=(n // tm,),
        in_specs=[tile, vec,
                  pl.BlockSpec((D_MODEL, 2 * FFN_W), lambda i: (0, 0)),
                  pl.BlockSpec((FFN_K, 2 * FFN_W), lambda i: (0, 0)),
                  pl.BlockSpec((FFN_W, D_MODEL), lambda i: (0, 0)),
                  vec],
        out_specs=tile,
        out_shape=jax.ShapeDtypeStruct((n, D_MODEL), F32),
        scratch_shapes=[pltpu.VMEM((tm, D_MODEL), BF16),
                        pltpu.VMEM((SUBLANES, 2 * FFN_W), F32),
                        pltpu.VMEM((tm, D_MODEL), F32)],
        compiler_params=_cparams(("arbitrary",)),
        name="conv_ffn",
    )(h2, g, w_up, w_dw, w_down, final_g)


def _tiles(seq):
    return dict(
        inproj_m=min(1024, seq), inproj_n=1024,
        conv_m=min(256, seq), pool_m=min(512, seq),
        attn_t=min(512, seq), merge_m=min(512, seq), ffn_m=min(256, seq),
    )


def kernel(x, rel_bias, norm1_g, w_in, conv_dw_w, conv_dw_b, conv_ln_g, conv_ln_b, lam_q1, lam_k1, lam_q2, lam_k2, subln_g, pool_w, pool_scale, w_br_conv, w_br_attn, w_br_pool, w_o, norm2_g, ffn_up, ffn_dw, ffn_down, final_g):
    batch, seq, d = x.shape
    depth = w_in.shape[0]
    assert d == D_MODEL and w_in.shape[2] == IN_COLS
    ts = _tiles(seq)
    assert seq % ts["attn_t"] == 0 and ts["attn_t"] >= LANES

    colscale = np.ones((1, IN_COLS), np.float32)
    colscale[:, COL_Q:COL_K] = HEAD_DIM ** -0.5 * LOG2E
    colscale = jnp.asarray(colscale)
    bias = _bias_tiles(rel_bias, seq, ts["attn_t"])
    row = lambda v: v.astype(F32).reshape(1, -1)

    x2 = x.reshape(batch * seq, d)
    for l in range(depth):
        lam_init = 0.8 - 0.6 * math.exp(-0.3 * l)
        z = _inproj(x2, row(norm1_g[l]), w_in[l].astype(BF16), colscale,
                    ts["inproj_m"], ts["inproj_n"])
        b_conv = _conv_branch(z, conv_dw_w[l].astype(F32), row(conv_dw_b[l]),
                              row(conv_ln_g[l]), row(conv_ln_b[l]), seq, ts["conv_m"])
        lam_vecs = jnp.stack([lam_q1[l], lam_k1[l], lam_q2[l], lam_k2[l]]).astype(F32)
        b_attn = _attention(z, bias, lam_vecs, row(subln_g[l]), batch, seq, ts["attn_t"], lam_init)
        b_pool = _pool_branch(z, pool_w[l].astype(BF16), row(pool_scale[l]), seq, ts["pool_m"])
        h2 = _merge(b_conv, b_attn, b_pool, z, x2,
                    w_br_conv[l].astype(BF16), w_br_attn[l].astype(BF16),
                    w_br_pool[l].astype(BF16), w_o[l].astype(BF16), ts["merge_m"])
        x2 = _ffn(h2, row(norm2_g[l]), ffn_up[l].astype(BF16), ffn_dw[l].astype(F32),
                  ffn_down[l].astype(BF16), row(final_g), seq, ts["ffn_m"],
                  final_norm=(l == depth - 1))
    return x2.reshape(batch, seq, d)
```

```python
import functools
import math

import jax
import jax.numpy as jnp
import numpy as np
from jax import lax
from jax.experimental import pallas as pl
from jax.experimental.pallas import tpu as pltpu

F32 = jnp.float32
BF16 = jnp.bfloat16

D_MODEL = 1024
N_HEADS = 8
HEAD_DIM = 64
V_DIM = 2 * HEAD_DIM
CONV_K = 31
POOL_WINDOWS = (2, 4, 8, 16)
POOL_GW = D_MODEL // len(POOL_WINDOWS)
FFN_W = 2816
FFN_K = 3
N_BUCKETS = 32
MAX_DIST = 128
EPS = 1e-6

COL_CONV = 0
COL_Q = 2 * D_MODEL
COL_K = COL_Q + D_MODEL
COL_V = COL_K + D_MODEL
COL_POOL = COL_V + D_MODEL
COL_GATE = COL_POOL + D_MODEL
IN_COLS = COL_GATE + 3 * D_MODEL

LANES = 128
SUBLANES = 8
VMEM_LIMIT = 56 * 1024 * 1024
NEG_BIG = -1e30
LOG2E = 1.4426950408889634
CONV_HALO = 32
POOL_HALO = 16
FFN_CHUNK = 256


def _cparams(sem):
    return pltpu.CompilerParams(dimension_semantics=sem, vmem_limit_bytes=VMEM_LIMIT)


def _rms(x, g):
    ms = jnp.mean(x * x, axis=-1, keepdims=True)
    return x * lax.rsqrt(ms + EPS) * g


def _inproj_kernel(x_ref, g_ref, w_ref, cs_ref, o_ref, xn_ref):
    @pl.when(pl.program_id(1) == 0)
    def _():
        xn_ref[...] = _rms(x_ref[...], g_ref[...]).astype(BF16)

    for c in range(w_ref.shape[1] // D_MODEL):
        cols = slice(c * D_MODEL, (c + 1) * D_MODEL)
        acc = jnp.dot(xn_ref[...], w_ref[:, cols], preferred_element_type=F32)
        o_ref[:, cols] = (acc * cs_ref[:, cols]).astype(BF16)


def _inproj(x2, g, w, colscale, tm, tn):
    n = x2.shape[0]
    return pl.pallas_call(
        _inproj_kernel,
        grid=(n // tm, IN_COLS // tn),
        in_specs=[
            pl.BlockSpec((tm, D_MODEL), lambda i, j: (i, 0)),
            pl.BlockSpec((1, D_MODEL), lambda i, j: (0, 0)),
            pl.BlockSpec((D_MODEL, tn), lambda i, j: (0, j)),
            pl.BlockSpec((1, tn), lambda i, j: (0, j)),
        ],
        out_specs=pl.BlockSpec((tm, tn), lambda i, j: (i, j)),
        out_shape=jax.ShapeDtypeStruct((n, IN_COLS), BF16),
        scratch_shapes=[pltpu.VMEM((tm, D_MODEL), BF16)],
        compiler_params=_cparams(("arbitrary", "arbitrary")),
        name="inproj",
    )(x2, g, w, colscale)


def _conv_kernel(a_ref, g_ref, ah_ref, gh_ref, w_ref, b_ref, lg_ref, lb_ref, o_ref,
                 h_ref, y_ref, *, tm, tiles_per_seq, rows):
    first = (pl.program_id(0) % tiles_per_seq) == 0
    n = tm + CONV_HALO
    main = a_ref[...].astype(F32) * jax.nn.sigmoid(g_ref[...].astype(F32))
    halo = ah_ref[...].astype(F32) * jax.nn.sigmoid(gh_ref[...].astype(F32))
    h = jnp.concatenate([jnp.where(first, 0.0, halo), main], axis=0)
    h_ref[0] = h
    for b in range(1, SUBLANES):
        h_ref[b] = pltpu.roll(h, n - b, 0)

    lead = CONV_HALO - (CONV_K - 1)

    def chunk(r, carry):
        r0 = pl.multiple_of(r * rows, rows)
        for cg in range(D_MODEL // LANES):
            cols = slice(cg * LANES, (cg + 1) * LANES)
            acc = jnp.broadcast_to(b_ref[:, cols], (rows, LANES))
            for j in range(CONV_K):
                off = lead + j
                start = pl.multiple_of(r0 + (off - off % SUBLANES), SUBLANES)
                acc = acc + w_ref[j:j + 1, cols] * h_ref[off % SUBLANES, pl.ds(start, rows), cols]
            y_ref[pl.ds(r0, rows), cols] = acc
        return carry

    lax.fori_loop(0, tm // rows, chunk, 0)

    y = y_ref[...]
    mu = jnp.mean(y, axis=-1, keepdims=True)
    yc = y - mu
    var = jnp.mean(yc * yc, axis=-1, keepdims=True)
    yn = yc * lax.rsqrt(var + EPS) * lg_ref[...] + lb_ref[...]
    o_ref[...] = (yn * jax.nn.sigmoid(yn)).astype(BF16)


def _conv_branch(z, w, b, lg, lb, seq, tm):
    n = z.shape[0]
    hb = tm // CONV_HALO
    halo_idx = lambda i: jnp.maximum(i * hb - 1, 0)
    kern = functools.partial(_conv_kernel, tm=tm, tiles_per_seq=seq // tm, rows=64)
    vec = pl.BlockSpec((1, D_MODEL), lambda i: (0, 0))
    return pl.pallas_call(
        kern,
        grid=(n // tm,),
        in_specs=[
            pl.BlockSpec((tm, D_MODEL), lambda i: (i, 0)),
            pl.BlockSpec((tm, D_MODEL), lambda i: (i, 1)),
            pl.BlockSpec((CONV_HALO, D_MODEL), lambda i: (halo_idx(i), 0)),
            pl.BlockSpec((CONV_HALO, D_MODEL), lambda i: (halo_idx(i), 1)),
            pl.BlockSpec((CONV_K, D_MODEL), lambda i: (0, 0)),
            vec, vec, vec,
        ],
        out_specs=pl.BlockSpec((tm, D_MODEL), lambda i: (i, 0)),
        out_shape=jax.ShapeDtypeStruct((n, D_MODEL), BF16),
        scratch_shapes=[pltpu.VMEM((SUBLANES, tm + CONV_HALO, D_MODEL), F32),
                        pltpu.VMEM((tm, D_MODEL), F32)],
        compiler_params=_cparams(("arbitrary",)),
        name="conv_branch",
    )(z, z, z, z, w, b, lg, lb)


def _attn_kernel(q_ref, k_ref, v_ref, bias_ref, lam_ref, sg_ref, o_ref,
                 s_ref, cm_ref, m_ref, l_ref, acc_ref, *, t, hps, lam_init):
    qi = pl.program_id(2)
    lane = lax.broadcasted_iota(jnp.int32, (t, V_DIM), 1)
    qmaps = []
    for hh in range(hps):
        q = q_ref[:, hh * V_DIM:(hh + 1) * V_DIM]
        zero = jnp.zeros_like(q)
        qmaps += [jnp.where(lane < HEAD_DIM, q, zero), jnp.where(lane >= HEAD_DIM, q, zero)]

    m_ref[...] = jnp.full(m_ref.shape, NEG_BIG, F32)
    l_ref[...] = jnp.zeros(l_ref.shape, F32)
    acc_ref[...] = jnp.zeros(acc_ref.shape, F32)

    def rows(ki):
        return pl.ds(pl.multiple_of(ki * t, t), t)

    def logits(ki):
        out = []
        for hh in range(hps):
            k = k_ref[rows(ki), hh * V_DIM:(hh + 1) * V_DIM]
            out += [lax.dot_general(k, qmaps[2 * hh + mi], (((1,), (1,)), ((), ())),
                                    preferred_element_type=F32) for mi in range(2)]
        return out

    def stash(s, which):
        for u in range(2 * hps):
            sb = s[u] if which is None else s[u] + bias_ref[u // 2, which]
            s_ref[u] = sb
            cm_ref[u] = jnp.max(sb, axis=0, keepdims=True)

    def accumulate(ki):
        for hh in range(hps):
            v = v_ref[rows(ki), hh * V_DIM:(hh + 1) * V_DIM]
            for u in (2 * hh, 2 * hh + 1):
                m_old = m_ref[u]
                m_new = jnp.maximum(m_old, cm_ref[u])
                alpha = jnp.exp2(m_old - m_new)
                p = jnp.exp2(s_ref[u] - m_new)
                l_ref[u] = alpha * l_ref[u] + jnp.sum(p, axis=0, keepdims=True)
                pv = lax.dot_general(v, p.astype(BF16), (((0,), (0,)), ((), ())),
                                     preferred_element_type=F32)
                acc_ref[u] = alpha * acc_ref[u] + pv
                m_ref[u] = m_new

    def step(ki, which):
        s_next = logits(ki + 1)
        accumulate(ki)
        stash(s_next, which)

    @pl.when(qi == 0)
    def _():
        stash(logits(0), 0)

    @pl.when(qi == 1)
    def _():
        stash(logits(0), 1)

    @pl.when(qi >= 2)
    def _():
        stash(logits(0), None)

    def far_step(ki, carry):
        step(ki, None)
        return carry

    lax.fori_loop(0, jnp.maximum(qi - 2, 0), far_step, 0)

    @pl.when(qi >= 2)
    def _():
        step(qi - 2, 1)

    @pl.when(qi >= 1)
    def _():
        step(qi - 1, 0)

    accumulate(qi)

    lam = (jnp.exp(jnp.sum(lam_ref[0:1, :] * lam_ref[1:2, :], axis=-1, keepdims=True))
           - jnp.exp(jnp.sum(lam_ref[2:3, :] * lam_ref[3:4, :], axis=-1, keepdims=True))
           + lam_init)
    for hh in range(hps):
        u = 2 * hh
        o_t = acc_ref[u] / l_ref[u] - lam * (acc_ref[u + 1] / l_ref[u + 1])
        o = o_t.T
        o_ref[:, hh * V_DIM:(hh + 1) * V_DIM] = (
            _rms(o, sg_ref[...]) * (1.0 - lam_init)).astype(BF16)


def _attention(z, bias, lam_vecs, subln_g, batch, seq, t, hps, lam_init):
    n = z.shape[0]
    nq = seq // t
    w = hps * V_DIM
    kern = functools.partial(_attn_kernel, t=t, hps=hps, lam_init=lam_init)
    return pl.pallas_call(
        kern,
        grid=(batch, N_HEADS // hps, nq),
        in_specs=[
            pl.BlockSpec((t, w), lambda b, h, i: (b * nq + i, COL_Q // w + h)),
            pl.BlockSpec((seq, w), lambda b, h, i: (b, COL_K // w + h)),
            pl.BlockSpec((seq, w), lambda b, h, i: (b, COL_V // w + h)),
            pl.BlockSpec((hps, 2, t, t), lambda b, h, i: (h, 0, 0, 0)),
            pl.BlockSpec((4, HEAD_DIM), lambda b, h, i: (0, 0)),
            pl.BlockSpec((1, V_DIM), lambda b, h, i: (0, 0)),
        ],
        out_specs=pl.BlockSpec((t, w), lambda b, h, i: (b * nq + i, h)),
        out_shape=jax.ShapeDtypeStruct((n, D_MODEL), BF16),
        scratch_shapes=[pltpu.VMEM((2 * hps, t, t), F32),
                        pltpu.VMEM((2 * hps, 1, t), F32),
                        pltpu.VMEM((2 * hps, 1, t), F32),
                        pltpu.VMEM((2 * hps, 1, t), F32),
                        pltpu.VMEM((2 * hps, V_DIM, t), F32)],
        compiler_params=_cparams(("arbitrary", "arbitrary", "arbitrary")),
        name="diff_attn",
    )(z, z, z, bias, lam_vecs, subln_g)


def _t5_buckets(n):
    rel = np.arange(n)
    max_exact = N_BUCKETS // 2
    nf = np.maximum(rel, 1).astype(np.float32)
    large = max_exact + (np.log(nf / np.float32(max_exact)) / np.float32(math.log(MAX_DIST / max_exact))
                         * np.float32(N_BUCKETS - max_exact)).astype(np.int32)
    large = np.minimum(large, N_BUCKETS - 1)
    return np.where(rel < max_exact, rel, large)


def _bias_kernel(rb_ref, idx_ref, o_ref, *, far_bucket):
    h = pl.program_id(0)
    far = rb_ref[far_bucket, h]
    for which in range(2):
        idx = idx_ref[which]
        acc = jnp.full(idx.shape, NEG_BIG, F32)
        for b in range(N_BUCKETS):
            acc = jnp.where(idx == b, (rb_ref[b, h] - far) * LOG2E, acc)
        o_ref[which] = acc


def _bias_tiles(rel_bias, seq, t):
    buckets = _t5_buckets(seq)
    assert np.all(buckets[t + 1:] == buckets[-1])
    jj, ii = np.meshgrid(np.arange(t), np.arange(t), indexing="ij")
    rel_d = ii - jj
    idx_d = np.where(rel_d >= 0, buckets[np.maximum(rel_d, 0)], N_BUCKETS)
    idx_s = buckets[rel_d + t]
    idx = jnp.asarray(np.stack([idx_d, idx_s]).astype(np.int32))
    kern = functools.partial(_bias_kernel, far_bucket=int(buckets[-1]))
    return pl.pallas_call(
        kern,
        grid=(N_HEADS,),
        in_specs=[pl.BlockSpec(memory_space=pltpu.SMEM),
                  pl.BlockSpec((2, t, t), lambda h: (0, 0, 0))],
        out_specs=pl.BlockSpec((None, 2, t, t), lambda h: (h, 0, 0, 0)),
        out_shape=jax.ShapeDtypeStruct((N_HEADS, 2, t, t), F32),
        compiler_params=_cparams(("arbitrary",)),
        name="bias_tiles",
    )(rel_bias.astype(F32), idx)


def _pool_kernel(p_ref, ph_ref, w_ref, sc_ref, o_ref, buf_ref, *, tm, tiles_per_seq):
    pos = pl.program_id(0) % tiles_per_seq
    first = pos == 0
    buf_ref[POOL_HALO:POOL_HALO + tm, :] = p_ref[...].astype(F32)
    buf_ref[0:POOL_HALO, :] = jnp.where(first, 0.0, ph_ref[...].astype(F32))
    t_idx = pos * tm + lax.broadcasted_iota(jnp.int32, (tm, POOL_GW), 0)
    for g, win in enumerate(POOL_WINDOWS):
        cols = slice(g * POOL_GW, (g + 1) * POOL_GW)
        cur = buf_ref[POOL_HALO:POOL_HALO + tm, cols]
        tot = cur
        for back in range(1, win):
            tot = tot + buf_ref[POOL_HALO - back:POOL_HALO - back + tm, cols]
        cnt = jnp.minimum(t_idx + 1, win).astype(F32)
        d = (tot / cnt - cur).astype(BF16)
        y = jnp.dot(d, w_ref[g], preferred_element_type=F32)
        o_ref[:, cols] = (y * sc_ref[:, cols]).astype(BF16)


def _pool_branch(z, w, scale, seq, tm):
    n = z.shape[0]
    hb = tm // POOL_HALO
    pc = COL_POOL // D_MODEL
    kern = functools.partial(_pool_kernel, tm=tm, tiles_per_seq=seq // tm)
    return pl.pallas_call(
        kern,
        grid=(n // tm,),
        in_specs=[
            pl.BlockSpec((tm, D_MODEL), lambda i: (i, pc)),
            pl.BlockSpec((POOL_HALO, D_MODEL), lambda i: (jnp.maximum(i * hb - 1, 0), pc)),
            pl.BlockSpec((len(POOL_WINDOWS), POOL_GW, POOL_GW), lambda i: (0, 0, 0)),
            pl.BlockSpec((1, D_MODEL), lambda i: (0, 0)),
        ],
        out_specs=pl.BlockSpec((tm, D_MODEL), lambda i: (i, 0)),
        out_shape=jax.ShapeDtypeStruct((n, D_MODEL), BF16),
        scratch_shapes=[pltpu.VMEM((tm + POOL_HALO, D_MODEL), F32)],
        compiler_params=_cparams(("arbitrary",)),
        name="pool_branch",
    )(z, z, w, scale)


def _merge_kernel(bc_ref, ba_ref, bp_ref, gt_ref, x_ref, wc_ref, wa_ref, wp_ref, wo_ref, o_ref):
    def gate(b):
        return jax.nn.sigmoid(gt_ref[:, b * D_MODEL:(b + 1) * D_MODEL].astype(F32))

    m = gate(0) * jnp.dot(bc_ref[...], wc_ref[...], preferred_element_type=F32)
    m = m + gate(1) * jnp.dot(ba_ref[...], wa_ref[...], preferred_element_type=F32)
    m = m + gate(2) * jnp.dot(bp_ref[...], wp_ref[...], preferred_element_type=F32)
    o_ref[...] = x_ref[...] + jnp.dot(m.astype(BF16), wo_ref[...], preferred_element_type=F32)


def _merge(b_conv, b_attn, b_pool, z, x2, wc, wa, wp, wo, tm):
    n = x2.shape[0]
    tile = pl.BlockSpec((tm, D_MODEL), lambda i: (i, 0))
    wspec = pl.BlockSpec((D_MODEL, D_MODEL), lambda i: (0, 0))
    return pl.pallas_call(
        _merge_kernel,
        grid=(n // tm,),
        in_specs=[tile, tile, tile,
                  pl.BlockSpec((tm, 3 * D_MODEL), lambda i: (i, COL_GATE // (3 * D_MODEL))),
                  tile, wspec, wspec, wspec, wspec],
        out_specs=tile,
        out_shape=jax.ShapeDtypeStruct((n, D_MODEL), F32),
        compiler_params=_cparams(("arbitrary",)),
        name="merge",
    )(b_conv, b_attn, b_pool, z, x2, wc, wa, wp, wo)


def _ffn_kernel(h_ref, g_ref, wu_ref, dw_ref, wd_ref, fg_ref, o_ref,
                xn_ref, carry_ref, acc_ref, *, tm, tiles_per_seq, final_norm):
    first = (pl.program_id(0) % tiles_per_seq) == 0

    @pl.when(first)
    def _():
        carry_ref[...] = jnp.zeros(carry_ref.shape, F32)

    h = h_ref[...]
    xn_ref[...] = _rms(h, g_ref[...]).astype(BF16)
    acc_ref[...] = jnp.zeros(acc_ref.shape, F32)

    def cols_of(c):
        return (slice(c * FFN_CHUNK, (c + 1) * FFN_CHUNK),
                slice(FFN_W + c * FFN_CHUNK, FFN_W + (c + 1) * FFN_CHUNK))

    def up(c):
        return [jnp.dot(xn_ref[...], wu_ref[:, cols], preferred_element_type=F32)
                for cols in cols_of(c)]

    def conv3(u, cols):
        ext = jnp.concatenate([carry_ref[:, cols], u], axis=0)
        carry_ref[:, cols] = u[tm - SUBLANES:, :]
        y = dw_ref[2:3, cols] * u
        for back in range(1, FFN_K):
            shifted = pltpu.roll(ext, back, 0)[SUBLANES:, :]
            y = y + dw_ref[FFN_K - 1 - back:FFN_K - back, cols] * shifted
        return y

    n_chunks = FFN_W // FFN_CHUNK
    ahead = 2
    pending = [up(c) for c in range(ahead)]
    for c in range(n_chunks):
        if c + ahead < n_chunks:
            pending.append(up(c + ahead))
        a, g = (conv3(u, cols) for u, cols in zip(pending.pop(0), cols_of(c)))
        mid = (g * jax.nn.sigmoid(g) * a).astype(BF16)
        acc_ref[...] += jnp.dot(mid, wd_ref[c * FFN_CHUNK:(c + 1) * FFN_CHUNK, :],
                                preferred_element_type=F32)

    out = h + acc_ref[...]
    if final_norm:
        out = _rms(out, fg_ref[...])
    o_ref[...] = out


def _ffn(h2, g, w_up, w_dw, w_down, final_g, seq, tm, final_norm):
    n = h2.shape[0]
    kern = functools.partial(_ffn_kernel, tm=tm, tiles_per_seq=seq // tm, final_norm=final_norm)
    tile = pl.BlockSpec((tm, D_MODEL), lambda i: (i, 0))
    vec = pl.BlockSpec((1, D_MODEL), lambda i: (0, 0))
    return pl.pallas_call(
        kern,
        grid=(n // tm,),
        in_specs=[tile, vec,
                  pl.BlockSpec((D_MODEL, 2 * FFN_W), lambda i: (0, 0)),
                  pl.BlockSpec((FFN_K, 2 * FFN_W), lambda i: (0, 0)),
                  pl.BlockSpec((FFN_W, D_MODEL), lambda i: (0, 0)),
                  vec],
        out_specs=tile,
        out_shape=jax.ShapeDtypeStruct((n, D_MODEL), F32),
        scratch_shapes=[pltpu.VMEM((tm, D_MODEL), BF16),
                        pltpu.VMEM((SUBLANES, 2 * FFN_W), F32),
                        pltpu.VMEM((tm, D_MODEL), F32)],
        compiler_params=_cparams(("arbitrary",)),
        name="conv_ffn",
    )(h2, g, w_up, w_dw, w_down, final_g)


def _tiles(seq):
    return dict(
        inproj_m=min(1024, seq), inproj_n=3 * D_MODEL,
        conv_m=min(256, seq), pool_m=min(512, seq),
        attn_t=min(512, seq), attn_heads=2, merge_m=min(512, seq), ffn_m=min(256, seq),
    )


def kernel(x, rel_bias, norm1_g, w_in, conv_dw_w, conv_dw_b, conv_ln_g, conv_ln_b, lam_q1, lam_k1, lam_q2, lam_k2, subln_g, pool_w, pool_scale, w_br_conv, w_br_attn, w_br_pool, w_o, norm2_g, ffn_up, ffn_dw, ffn_down, final_g):
    batch, seq, d = x.shape
    depth = w_in.shape[0]
    assert d == D_MODEL and w_in.shape[2] == IN_COLS
    ts = _tiles(seq)
    assert seq % ts["attn_t"] == 0 and ts["attn_t"] >= LANES

    colscale = np.ones((1, IN_COLS), np.float32)
    colscale[:, COL_Q:COL_K] = HEAD_DIM ** -0.5 * LOG2E
    colscale = jnp.asarray(colscale)
    bias = _bias_tiles(rel_bias, seq, ts["attn_t"])
    row = lambda v: v.astype(F32).reshape(1, -1)

    x2 = x.reshape(batch * seq, d)
    for l in range(depth):
        lam_init = 0.8 - 0.6 * math.exp(-0.3 * l)
        z = _inproj(x2, row(norm1_g[l]), w_in[l].astype(BF16), colscale,
                    ts["inproj_m"], ts["inproj_n"])
        b_conv = _conv_branch(z, conv_dw_w[l].astype(F32), row(conv_dw_b[l]),
                              row(conv_ln_g[l]), row(conv_ln_b[l]), seq, ts["conv_m"])
        lam_vecs = jnp.stack([lam_q1[l], lam_k1[l], lam_q2[l], lam_k2[l]]).astype(F32)
        b_attn = _attention(z, bias, lam_vecs, row(subln_g[l]), batch, seq, ts["attn_t"],
                            ts["attn_heads"], lam_init)
        b_pool = _pool_branch(z, pool_w[l].astype(BF16), row(pool_scale[l]), seq, ts["pool_m"])
        h2 = _merge(b_conv, b_attn, b_pool, z, x2,
                    w_br_conv[l].astype(BF16), w_br_attn[l].astype(BF16),
                    w_br_pool[l].astype(BF16), w_o[l].astype(BF16), ts["merge_m"])
        x2 = _ffn(h2, row(norm2_g[l]), ffn_up[l].astype(BF16), ffn_dw[l].astype(F32),
                  ffn_down[l].astype(BF16), row(final_g), seq, ts["ffn_m"],
                  final_norm=(l == depth - 1))
    return x2.reshape(batch, seq, d)
```

```python
import functools
import math

import jax
import jax.numpy as jnp
import numpy as np
from jax import lax
from jax.experimental import pallas as pl
from jax.experimental.pallas import tpu as pltpu

F32 = jnp.float32
BF16 = jnp.bfloat16

D_MODEL = 1024
N_HEADS = 8
HEAD_DIM = 64
V_DIM = 2 * HEAD_DIM
CONV_K = 31
POOL_WINDOWS = (2, 4, 8, 16)
POOL_GW = D_MODEL // len(POOL_WINDOWS)
FFN_W = 2816
FFN_K = 3
N_BUCKETS = 32
MAX_DIST = 128
EPS = 1e-6

COL_CONV = 0
COL_Q = 2 * D_MODEL
COL_K = COL_Q + D_MODEL
COL_V = COL_K + D_MODEL
COL_POOL = COL_V + D_MODEL
COL_GATE = COL_POOL + D_MODEL
IN_COLS = COL_GATE + 3 * D_MODEL

LANES = 128
SUBLANES = 8
VMEM_LIMIT = 56 * 1024 * 1024
NEG_BIG = -1e30
LOG2E = 1.4426950408889634
CONV_HALO = 32
POOL_HALO = 16
FFN_CHUNK = 256

def _cparams(sem):
    return pltpu.CompilerParams(dimension_semantics=sem, vmem_limit_bytes=VMEM_LIMIT)


def _rms(x, g):
    ms = jnp.mean(x * x, axis=-1, keepdims=True)
    return x * lax.rsqrt(ms + EPS) * g


def _inproj_kernel(x_ref, g_ref, w_ref, cs_ref, o_ref, xn_ref):
    @pl.when(pl.program_id(1) == 0)
    def _():
        xn_ref[...] = _rms(x_ref[...], g_ref[...]).astype(BF16)

    for c in range(w_ref.shape[1] // D_MODEL):
        cols = slice(c * D_MODEL, (c + 1) * D_MODEL)
        acc = jnp.dot(xn_ref[...], w_ref[:, cols], preferred_element_type=F32)
        o_ref[:, cols] = (acc * cs_ref[:, cols]).astype(BF16)


def _inproj(x2, g, w, colscale, tm, tn):
    n = x2.shape[0]
    return pl.pallas_call(
        _inproj_kernel,
        grid=(n // tm, IN_COLS // tn),
        in_specs=[
            pl.BlockSpec((tm, D_MODEL), lambda i, j: (i, 0)),
            pl.BlockSpec((1, D_MODEL), lambda i, j: (0, 0)),
            pl.BlockSpec((D_MODEL, tn), lambda i, j: (0, j)),
            pl.BlockSpec((1, tn), lambda i, j: (0, j)),
        ],
        out_specs=pl.BlockSpec((tm, tn), lambda i, j: (i, j)),
        out_shape=jax.ShapeDtypeStruct((n, IN_COLS), BF16),
        scratch_shapes=[pltpu.VMEM((tm, D_MODEL), BF16)],
        compiler_params=_cparams(("arbitrary", "arbitrary")),
        name="inproj",
    )(x2, g, w, colscale)


def _conv_kernel(a_ref, g_ref, ah_ref, gh_ref, w_ref, b_ref, lg_ref, lb_ref, o_ref,
                 h_ref, y_ref, *, tm, tiles_per_seq, rows):
    first = (pl.program_id(0) % tiles_per_seq) == 0
    n = tm + CONV_HALO
    main = a_ref[...].astype(F32) * jax.nn.sigmoid(g_ref[...].astype(F32))
    halo = ah_ref[...].astype(F32) * jax.nn.sigmoid(gh_ref[...].astype(F32))
    h = jnp.concatenate([jnp.where(first, 0.0, halo), main], axis=0)
    h_ref[0] = h
    for b in range(1, SUBLANES):
        h_ref[b] = pltpu.roll(h, n - b, 0)

    lead = CONV_HALO - (CONV_K - 1)

    def chunk(r, carry):
        r0 = pl.multiple_of(r * rows, rows)
        for cg in range(D_MODEL // LANES):
            cols = slice(cg * LANES, (cg + 1) * LANES)
            acc = jnp.broadcast_to(b_ref[:, cols], (rows, LANES))
            for b in range(SUBLANES):
                hb = h_ref[b, pl.ds(r0, rows + CONV_HALO), cols]
                for off in range(b, CONV_HALO + 1, SUBLANES):
                    j = off - lead
                    if 0 <= j < CONV_K:
                        acc = acc + w_ref[j:j + 1, cols] * hb[off - b:off - b + rows]
            y_ref[pl.ds(r0, rows), cols] = acc
        return carry

    lax.fori_loop(0, tm // rows, chunk, 0)

    y = y_ref[...]
    mu = jnp.mean(y, axis=-1, keepdims=True)
    yc = y - mu
    var = jnp.mean(yc * yc, axis=-1, keepdims=True)
    yn = yc * lax.rsqrt(var + EPS) * lg_ref[...] + lb_ref[...]
    o_ref[...] = (yn * jax.nn.sigmoid(yn)).astype(BF16)


def _conv_branch(z, w, b, lg, lb, seq, tm):
    n = z.shape[0]
    hb = tm // CONV_HALO
    halo_idx = lambda i: jnp.maximum(i * hb - 1, 0)
    kern = functools.partial(_conv_kernel, tm=tm, tiles_per_seq=seq // tm, rows=32)
    vec = pl.BlockSpec((1, D_MODEL), lambda i: (0, 0))
    return pl.pallas_call(
        kern,
        grid=(n // tm,),
        in_specs=[
            pl.BlockSpec((tm, D_MODEL), lambda i: (i, 0)),
            pl.BlockSpec((tm, D_MODEL), lambda i: (i, 1)),
            pl.BlockSpec((CONV_HALO, D_MODEL), lambda i: (halo_idx(i), 0)),
            pl.BlockSpec((CONV_HALO, D_MODEL), lambda i: (halo_idx(i), 1)),
            pl.BlockSpec((CONV_K, D_MODEL), lambda i: (0, 0)),
            vec, vec, vec,
        ],
        out_specs=pl.BlockSpec((tm, D_MODEL), lambda i: (i, 0)),
        out_shape=jax.ShapeDtypeStruct((n, D_MODEL), BF16),
        scratch_shapes=[pltpu.VMEM((SUBLANES, tm + CONV_HALO, D_MODEL), F32),
                        pltpu.VMEM((tm, D_MODEL), F32)],
        compiler_params=_cparams(("arbitrary",)),
        name="conv_branch",
    )(z, z, z, z, w, b, lg, lb)


def _attn_kernel(q_ref, k_ref, v_ref, bias_ref, lam_ref, sg_ref, o_ref,
                 s_ref, cm_ref, m_ref, l_ref, acc_ref, *, t, hps, lam_init):
    qi = pl.program_id(2)
    lane = lax.broadcasted_iota(jnp.int32, (t, V_DIM), 1)
    qmaps = []
    for hh in range(hps):
        q = q_ref[:, hh * V_DIM:(hh + 1) * V_DIM]
        zero = jnp.zeros_like(q)
        qmaps += [jnp.where(lane < HEAD_DIM, q, zero), jnp.where(lane >= HEAD_DIM, q, zero)]

    m_ref[...] = jnp.full(m_ref.shape, NEG_BIG, F32)
    l_ref[...] = jnp.zeros(l_ref.shape, F32)
    acc_ref[...] = jnp.zeros(acc_ref.shape, F32)

    def rows(ki):
        return pl.ds(pl.multiple_of(ki * t, t), t)

    def logits(ki):
        out = []
        for hh in range(hps):
            k = k_ref[rows(ki), hh * V_DIM:(hh + 1) * V_DIM]
            out += [lax.dot_general(k, qmaps[2 * hh + mi], (((1,), (1,)), ((), ())),
                                    preferred_element_type=F32) for mi in range(2)]
        return out

    def stash(ki, buf, which):
        s = logits(ki)
        for u in range(2 * hps):
            sb = s[u] if which is None else s[u] + bias_ref[u // 2, which]
            s_ref[buf, u] = sb
            cm_ref[buf, u] = jnp.max(sb, axis=0, keepdims=True)

    def accumulate(ki, buf):
        for hh in range(hps):
            v = v_ref[rows(ki), hh * V_DIM:(hh + 1) * V_DIM]
            for u in (2 * hh, 2 * hh + 1):
                m_old = m_ref[u]
                m_new = jnp.maximum(m_old, cm_ref[buf, u])
                alpha = jnp.exp2(m_old - m_new)
                p = jnp.exp2(s_ref[buf, u] - m_new)
                l_ref[u] = alpha * l_ref[u] + jnp.sum(p, axis=0, keepdims=True)
                pv = lax.dot_general(v, p.astype(BF16), (((0,), (0,)), ((), ())),
                                     preferred_element_type=F32)
                acc_ref[u] = alpha * acc_ref[u] + pv
                m_ref[u] = m_new

    def step(ki, buf, which):
        stash(ki + 1, 1 - buf, which)
        accumulate(ki, buf)

    n = jnp.maximum(qi - 1, 0)
    odd = (n - 1) % 2 == 1

    @pl.when(qi == 0)
    def _():
        stash(0, 0, 0)

    @pl.when(qi == 1)
    def _():
        stash(0, 1, 1)

    @pl.when(jnp.logical_and(qi >= 2, jnp.logical_not(odd)))
    def _():
        stash(0, 0, None)

    @pl.when(jnp.logical_and(qi >= 2, odd))
    def _():
        stash(0, 1, None)
        step(0, 1, None)

    def far_pair(j, carry):
        base = jnp.where(odd, 1, 0) + 2 * j
        step(base, 0, None)
        step(base + 1, 1, None)
        return carry

    lax.fori_loop(0, jnp.maximum(n - 1, 0) // 2, far_pair, 0)

    @pl.when(qi >= 2)
    def _():
        step(qi - 2, 0, 1)

    @pl.when(qi >= 1)
    def _():
        step(qi - 1, 1, 0)

    accumulate(qi, 0)

    lam = (jnp.exp(jnp.sum(lam_ref[0:1, :] * lam_ref[1:2, :], axis=-1, keepdims=True))
           - jnp.exp(jnp.sum(lam_ref[2:3, :] * lam_ref[3:4, :], axis=-1, keepdims=True))
           + lam_init)
    for hh in range(hps):
        u = 2 * hh
        o_t = acc_ref[u] / l_ref[u] - lam * (acc_ref[u + 1] / l_ref[u + 1])
        o = o_t.T
        o_ref[:, hh * V_DIM:(hh + 1) * V_DIM] = (
            _rms(o, sg_ref[...]) * (1.0 - lam_init)).astype(BF16)


def _attention(z, bias, lam_vecs, subln_g, batch, seq, t, hps, lam_init):
    n = z.shape[0]
    nq = seq // t
    w = hps * V_DIM
    kern = functools.partial(_attn_kernel, t=t, hps=hps, lam_init=lam_init)
    return pl.pallas_call(
        kern,
        grid=(batch, N_HEADS // hps, nq),
        in_specs=[
            pl.BlockSpec((t, w), lambda b, h, i: (b * nq + i, COL_Q // w + h)),
            pl.BlockSpec((seq, w), lambda b, h, i: (b, COL_K // w + h)),
            pl.BlockSpec((seq, w), lambda b, h, i: (b, COL_V // w + h)),
            pl.BlockSpec((hps, 2, t, t), lambda b, h, i: (h, 0, 0, 0)),
            pl.BlockSpec((4, HEAD_DIM), lambda b, h, i: (0, 0)),
            pl.BlockSpec((1, V_DIM), lambda b, h, i: (0, 0)),
        ],
        out_specs=pl.BlockSpec((t, w), lambda b, h, i: (b * nq + i, h)),
        out_shape=jax.ShapeDtypeStruct((n, D_MODEL), BF16),
        scratch_shapes=[pltpu.VMEM((2, 2 * hps, t, t), F32),
                        pltpu.VMEM((2, 2 * hps, 1, t), F32),
                        pltpu.VMEM((2 * hps, 1, t), F32),
                        pltpu.VMEM((2 * hps, 1, t), F32),
                        pltpu.VMEM((2 * hps, V_DIM, t), F32)],
        compiler_params=_cparams(("arbitrary", "arbitrary", "arbitrary")),
        name="diff_attn",
    )(z, z, z, bias, lam_vecs, subln_g)


def _t5_buckets(n):
    rel = np.arange(n)
    max_exact = N_BUCKETS // 2
    nf = np.maximum(rel, 1).astype(np.float32)
    large = max_exact + (np.log(nf / np.float32(max_exact)) / np.float32(math.log(MAX_DIST / max_exact))
                         * np.float32(N_BUCKETS - max_exact)).astype(np.int32)
    large = np.minimum(large, N_BUCKETS - 1)
    return np.where(rel < max_exact, rel, large)


def _bias_kernel(rb_ref, idx_ref, o_ref, *, far_bucket):
    h = pl.program_id(0)
    far = rb_ref[far_bucket, h]
    for which in range(2):
        idx = idx_ref[which]
        acc = jnp.full(idx.shape, NEG_BIG, F32)
        for b in range(N_BUCKETS):
            acc = jnp.where(idx == b, (rb_ref[b, h] - far) * LOG2E, acc)
        o_ref[which] = acc


def _bias_tiles(rel_bias, seq, t):
    buckets = _t5_buckets(seq)
    assert np.all(buckets[t + 1:] == buckets[-1])
    jj, ii = np.meshgrid(np.arange(t), np.arange(t), indexing="ij")
    rel_d = ii - jj
    idx_d = np.where(rel_d >= 0, buckets[np.maximum(rel_d, 0)], N_BUCKETS)
    idx_s = buckets[rel_d + t]
    idx = jnp.asarray(np.stack([idx_d, idx_s]).astype(np.int32))
    kern = functools.partial(_bias_kernel, far_bucket=int(buckets[-1]))
    return pl.pallas_call(
        kern,
        grid=(N_HEADS,),
        in_specs=[pl.BlockSpec(memory_space=pltpu.SMEM),
                  pl.BlockSpec((2, t, t), lambda h: (0, 0, 0))],
        out_specs=pl.BlockSpec((None, 2, t, t), lambda h: (h, 0, 0, 0)),
        out_shape=jax.ShapeDtypeStruct((N_HEADS, 2, t, t), F32),
        compiler_params=_cparams(("arbitrary",)),
        name="bias_tiles",
    )(rel_bias.astype(F32), idx)


def _pool_kernel(p_ref, ph_ref, w_ref, sc_ref, o_ref, buf_ref, *, tm, tiles_per_seq):
    pos = pl.program_id(0) % tiles_per_seq
    first = pos == 0
    buf_ref[POOL_HALO:POOL_HALO + tm, :] = p_ref[...].astype(F32)
    buf_ref[0:POOL_HALO, :] = jnp.where(first, 0.0, ph_ref[...].astype(F32))
    t_idx = pos * tm + lax.broadcasted_iota(jnp.int32, (tm, POOL_GW), 0)
    for g, win in enumerate(POOL_WINDOWS):
        cols = slice(g * POOL_GW, (g + 1) * POOL_GW)
        cur = buf_ref[POOL_HALO:POOL_HALO + tm, cols]
        tot = cur
        for back in range(1, win):
            tot = tot + buf_ref[POOL_HALO - back:POOL_HALO - back + tm, cols]
        cnt = jnp.minimum(t_idx + 1, win).astype(F32)
        d = (tot / cnt - cur).astype(BF16)
        y = jnp.dot(d, w_ref[g], preferred_element_type=F32)
        o_ref[:, cols] = (y * sc_ref[:, cols]).astype(BF16)


def _pool_branch(z, w, scale, seq, tm):
    n = z.shape[0]
    hb = tm // POOL_HALO
    pc = COL_POOL // D_MODEL
    kern = functools.partial(_pool_kernel, tm=tm, tiles_per_seq=seq // tm)
    return pl.pallas_call(
        kern,
        grid=(n // tm,),
        in_specs=[
            pl.BlockSpec((tm, D_MODEL), lambda i: (i, pc)),
            pl.BlockSpec((POOL_HALO, D_MODEL), lambda i: (jnp.maximum(i * hb - 1, 0), pc)),
            pl.BlockSpec((len(POOL_WINDOWS), POOL_GW, POOL_GW), lambda i: (0, 0, 0)),
            pl.BlockSpec((1, D_MODEL), lambda i: (0, 0)),
        ],
        out_specs=pl.BlockSpec((tm, D_MODEL), lambda i: (i, 0)),
        out_shape=jax.ShapeDtypeStruct((n, D_MODEL), BF16),
        scratch_shapes=[pltpu.VMEM((tm + POOL_HALO, D_MODEL), F32)],
        compiler_params=_cparams(("arbitrary",)),
        name="pool_branch",
    )(z, z, w, scale)


def _merge_kernel(bc_ref, ba_ref, bp_ref, gt_ref, x_ref, wc_ref, wa_ref, wp_ref, wo_ref, o_ref):
    def gate(b):
        return jax.nn.sigmoid(gt_ref[:, b * D_MODEL:(b + 1) * D_MODEL].astype(F32))

    m = gate(0) * jnp.dot(bc_ref[...], wc_ref[...], preferred_element_type=F32)
    m = m + gate(1) * jnp.dot(ba_ref[...], wa_ref[...], preferred_element_type=F32)
    m = m + gate(2) * jnp.dot(bp_ref[...], wp_ref[...], preferred_element_type=F32)
    o_ref[...] = x_ref[...] + jnp.dot(m.astype(BF16), wo_ref[...], preferred_element_type=F32)


def _merge(b_conv, b_attn, b_pool, z, x2, wc, wa, wp, wo, tm):
    n = x2.shape[0]
    tile = pl.BlockSpec((tm, D_MODEL), lambda i: (i, 0))
    wspec = pl.BlockSpec((D_MODEL, D_MODEL), lambda i: (0, 0))
    return pl.pallas_call(
        _merge_kernel,
        grid=(n // tm,),
        in_specs=[tile, tile, tile,
                  pl.BlockSpec((tm, 3 * D_MODEL), lambda i: (i, COL_GATE // (3 * D_MODEL))),
                  tile, wspec, wspec, wspec, wspec],
        out_specs=tile,
        out_shape=jax.ShapeDtypeStruct((n, D_MODEL), F32),
        compiler_params=_cparams(("arbitrary",)),
        name="merge",
    )(b_conv, b_attn, b_pool, z, x2, wc, wa, wp, wo)


def _ffn_kernel(h_ref, g_ref, wu_ref, dw_ref, wd_ref, fg_ref, o_ref,
                xn_ref, carry_ref, acc_ref, *, tm, tiles_per_seq, final_norm):
    first = (pl.program_id(0) % tiles_per_seq) == 0

    @pl.when(first)
    def _():
        carry_ref[...] = jnp.zeros(carry_ref.shape, F32)

    h = h_ref[...]
    xn_ref[...] = _rms(h, g_ref[...]).astype(BF16)
    acc_ref[...] = jnp.zeros(acc_ref.shape, F32)

    def cols_of(c):
        return (slice(c * FFN_CHUNK, (c + 1) * FFN_CHUNK),
                slice(FFN_W + c * FFN_CHUNK, FFN_W + (c + 1) * FFN_CHUNK))

    def up(c):
        return [jnp.dot(xn_ref[...], wu_ref[:, cols], preferred_element_type=F32)
                for cols in cols_of(c)]

    def conv3(u, cols):
        ext = jnp.concatenate([carry_ref[:, cols], u], axis=0)
        carry_ref[:, cols] = u[tm - SUBLANES:, :]
        y = dw_ref[2:3, cols] * u
        for back in range(1, FFN_K):
            shifted = pltpu.roll(ext, back, 0)[SUBLANES:, :]
            y = y + dw_ref[FFN_K - 1 - back:FFN_K - back, cols] * shifted
        return y

    n_chunks = FFN_W // FFN_CHUNK
    ahead = 2
    pending = [up(c) for c in range(ahead)]
    for c in range(n_chunks):
        if c + ahead < n_chunks:
            pending.append(up(c + ahead))
        a, g = (conv3(u, cols) for u, cols in zip(pending.pop(0), cols_of(c)))
        mid = (g * jax.nn.sigmoid(g) * a).astype(BF16)
        acc_ref[...] += jnp.dot(mid, wd_ref[c * FFN_CHUNK:(c + 1) * FFN_CHUNK, :],
                                preferred_element_type=F32)

    out = h + acc_ref[...]
    if final_norm:
        out = _rms(out, fg_ref[...])
    o_ref[...] = out


def _ffn(h2, g, w_up, w_dw, w_down, final_g, seq, tm, final_norm):
    n = h2.shape[0]
    kern = functools.partial(_ffn_kernel, tm=tm, tiles_per_seq=seq // tm, final_norm=final_norm)
    tile = pl.BlockSpec((tm, D_MODEL), lambda i: (i, 0))
    vec = pl.BlockSpec((1, D_MODEL), lambda i: (0, 0))
    return pl.pallas_call(
        kern,
        grid=(n // tm,),
        in_specs=[tile, vec,
                  pl.BlockSpec((D_MODEL, 2 * FFN_W), lambda i: (0, 0)),
                  pl.BlockSpec((FFN_K, 2 * FFN_W), lambda i: (0, 0)),
                  pl.BlockSpec((FFN_W, D_MODEL), lambda i: (0, 0)),
                  vec],
        out_specs=tile,
        out_shape=jax.ShapeDtypeStruct((n, D_MODEL), F32),
        scratch_shapes=[pltpu.VMEM((tm, D_MODEL), BF16),
                        pltpu.VMEM((SUBLANES, 2 * FFN_W), F32),
                        pltpu.VMEM((tm, D_MODEL), F32)],
        compiler_params=_cparams(("arbitrary",)),
        name="conv_ffn",
    )(h2, g, w_up, w_dw, w_down, final_g)


def _tiles(seq):
    return dict(
        inproj_m=min(1024, seq), inproj_n=3 * D_MODEL,
        conv_m=min(256, seq), pool_m=min(512, seq),
        attn_t=min(512, seq), attn_heads=2, merge_m=min(512, seq), ffn_m=min(256, seq),
    )


def kernel(x, rel_bias, norm1_g, w_in, conv_dw_w, conv_dw_b, conv_ln_g, conv_ln_b, lam_q1, lam_k1, lam_q2, lam_k2, subln_g, pool_w, pool_scale, w_br_conv, w_br_attn, w_br_pool, w_o, norm2_g, ffn_up, ffn_dw, ffn_down, final_g):
    batch, seq, d = x.shape
    depth = w_in.shape[0]
    assert d == D_MODEL and w_in.shape[2] == IN_COLS
    ts = _tiles(seq)
    assert seq % ts["attn_t"] == 0 and ts["attn_t"] >= LANES

    colscale = np.ones((1, IN_COLS), np.float32)
    colscale[:, COL_Q:COL_K] = HEAD_DIM ** -0.5 * LOG2E
    colscale = jnp.asarray(colscale)
    bias = _bias_tiles(rel_bias, seq, ts["attn_t"])
    row = lambda v: v.astype(F32).reshape(1, -1)

    x2 = x.reshape(batch * seq, d)
    for l in range(depth):
        lam_init = 0.8 - 0.6 * math.exp(-0.3 * l)
        z = _inproj(x2, row(norm1_g[l]), w_in[l].astype(BF16), colscale,
                    ts["inproj_m"], ts["inproj_n"])
        b_conv = _conv_branch(z, conv_dw_w[l].astype(F32), row(conv_dw_b[l]),
                              row(conv_ln_g[l]), row(conv_ln_b[l]), seq, ts["conv_m"])
        lam_vecs = jnp.stack([lam_q1[l], lam_k1[l], lam_q2[l], lam_k2[l]]).astype(F32)
        b_attn = _attention(z, bias, lam_vecs, row(subln_g[l]), batch, seq, ts["attn_t"],
                            ts["attn_heads"], lam_init)
        b_pool = _pool_branch(z, pool_w[l].astype(BF16), row(pool_scale[l]), seq, ts["pool_m"])
        h2 = _merge(b_conv, b_attn, b_pool, z, x2,
                    w_br_conv[l].astype(BF16), w_br_attn[l].astype(BF16),
                    w_br_pool[l].astype(BF16), w_o[l].astype(BF16), ts["merge_m"])
        x2 = _ffn(h2, row(norm2_g[l]), ffn_up[l].astype(BF16), ffn_dw[l].astype(F32),
                  ffn_down[l].astype(BF16), row(final_g), seq, ts["ffn_m"],
                  final_norm=(l == depth - 1))
    return x2.reshape(batch, seq, d)
```

```python
import functools
import math

import jax
import jax.numpy as jnp
import numpy as np
from jax import lax
from jax.experimental import pallas as pl
from jax.experimental.pallas import tpu as pltpu

F32 = jnp.float32
BF16 = jnp.bfloat16

D_MODEL = 1024
N_HEADS = 8
HEAD_DIM = 64
V_DIM = 2 * HEAD_DIM
CONV_K = 31
POOL_WINDOWS = (2, 4, 8, 16)
POOL_GW = D_MODEL // len(POOL_WINDOWS)
FFN_W = 2816
FFN_K = 3
N_BUCKETS = 32
MAX_DIST = 128
EPS = 1e-6

COL_CONV = 0
COL_Q = 2 * D_MODEL
COL_K = COL_Q + D_MODEL
COL_V = COL_K + D_MODEL
COL_POOL = COL_V + D_MODEL
COL_GATE = COL_POOL + D_MODEL
IN_COLS = COL_GATE + 3 * D_MODEL

LANES = 128
SUBLANES = 8
VMEM_LIMIT = 56 * 1024 * 1024
NEG_BIG = -1e30
F32_HUGE = 3.0e38
ATTN_FAR_GROUP = 4
LOG2E = 1.4426950408889634
CONV_HALO = 32
POOL_HALO = 16
FFN_CHUNK = 256

def _cparams(sem):
    return pltpu.CompilerParams(dimension_semantics=sem, vmem_limit_bytes=VMEM_LIMIT)


def _rms(x, g):
    ms = jnp.mean(x * x, axis=-1, keepdims=True)
    return x * lax.rsqrt(ms + EPS) * g


def _inproj_kernel(x_ref, g_ref, w_ref, cs_ref, o_ref, xn_ref):
    @pl.when(pl.program_id(1) == 0)
    def _():
        xn_ref[...] = _rms(x_ref[...], g_ref[...]).astype(BF16)

    for c in range(w_ref.shape[1] // D_MODEL):
        cols = slice(c * D_MODEL, (c + 1) * D_MODEL)
        acc = jnp.dot(xn_ref[...], w_ref[:, cols], preferred_element_type=F32)
        o_ref[:, cols] = (acc * cs_ref[:, cols]).astype(BF16)


def _inproj(x2, g, w, colscale, tm, tn):
    n = x2.shape[0]
    return pl.pallas_call(
        _inproj_kernel,
        grid=(n // tm, IN_COLS // tn),
        in_specs=[
            pl.BlockSpec((tm, D_MODEL), lambda i, j: (i, 0)),
            pl.BlockSpec((1, D_MODEL), lambda i, j: (0, 0)),
            pl.BlockSpec((D_MODEL, tn), lambda i, j: (0, j)),
            pl.BlockSpec((1, tn), lambda i, j: (0, j)),
        ],
        out_specs=pl.BlockSpec((tm, tn), lambda i, j: (i, j)),
        out_shape=jax.ShapeDtypeStruct((n, IN_COLS), BF16),
        scratch_shapes=[pltpu.VMEM((tm, D_MODEL), BF16)],
        compiler_params=_cparams(("arbitrary", "arbitrary")),
        name="inproj",
    )(x2, g, w, colscale)


def _conv_kernel(a_ref, g_ref, ah_ref, gh_ref, w_ref, b_ref, lg_ref, lb_ref, o_ref,
                 h_ref, y_ref, *, tm, tiles_per_seq, rows):
    first = (pl.program_id(0) % tiles_per_seq) == 0
    n = tm + CONV_HALO
    main = a_ref[...].astype(F32) * jax.nn.sigmoid(g_ref[...].astype(F32))
    halo = ah_ref[...].astype(F32) * jax.nn.sigmoid(gh_ref[...].astype(F32))
    h = jnp.concatenate([jnp.where(first, 0.0, halo), main], axis=0)
    h_ref[0] = h
    for b in range(1, SUBLANES):
        h_ref[b] = pltpu.roll(h, n - b, 0)

    lead = CONV_HALO - (CONV_K - 1)

    def chunk(r, carry):
        r0 = pl.multiple_of(r * rows, rows)
        for cg in range(D_MODEL // LANES):
            cols = slice(cg * LANES, (cg + 1) * LANES)
            acc = jnp.broadcast_to(b_ref[:, cols], (rows, LANES))
            for b in range(SUBLANES):
                hb = h_ref[b, pl.ds(r0, rows + CONV_HALO), cols]
                for off in range(b, CONV_HALO + 1, SUBLANES):
                    j = off - lead
                    if 0 <= j < CONV_K:
                        acc = acc + w_ref[j:j + 1, cols] * hb[off - b:off - b + rows]
            y_ref[pl.ds(r0, rows), cols] = acc
        return carry

    lax.fori_loop(0, tm // rows, chunk, 0)

    y = y_ref[...]
    mu = jnp.mean(y, axis=-1, keepdims=True)
    yc = y - mu
    var = jnp.mean(yc * yc, axis=-1, keepdims=True)
    yn = yc * lax.rsqrt(var + EPS) * lg_ref[...] + lb_ref[...]
    o_ref[...] = (yn * jax.nn.sigmoid(yn)).astype(BF16)


def _conv_branch(z, w, b, lg, lb, seq, tm):
    n = z.shape[0]
    hb = tm // CONV_HALO
    halo_idx = lambda i: jnp.maximum(i * hb - 1, 0)
    kern = functools.partial(_conv_kernel, tm=tm, tiles_per_seq=seq // tm, rows=32)
    vec = pl.BlockSpec((1, D_MODEL), lambda i: (0, 0))
    return pl.pallas_call(
        kern,
        grid=(n // tm,),
        in_specs=[
            pl.BlockSpec((tm, D_MODEL), lambda i: (i, 0)),
            pl.BlockSpec((tm, D_MODEL), lambda i: (i, 1)),
            pl.BlockSpec((CONV_HALO, D_MODEL), lambda i: (halo_idx(i), 0)),
            pl.BlockSpec((CONV_HALO, D_MODEL), lambda i: (halo_idx(i), 1)),
            pl.BlockSpec((CONV_K, D_MODEL), lambda i: (0, 0)),
            vec, vec, vec,
        ],
        out_specs=pl.BlockSpec((tm, D_MODEL), lambda i: (i, 0)),
        out_shape=jax.ShapeDtypeStruct((n, D_MODEL), BF16),
        scratch_shapes=[pltpu.VMEM((SUBLANES, tm + CONV_HALO, D_MODEL), F32),
                        pltpu.VMEM((tm, D_MODEL), F32)],
        compiler_params=_cparams(("arbitrary",)),
        name="conv_branch",
    )(z, z, z, z, w, b, lg, lb)


def _attn_kernel(q_ref, k_ref, v_ref, bias_ref, lam_ref, sg_ref, o_ref,
                 r_ref, l_ref, acc_ref, *, t, hps, lam_init):
    qi = pl.program_id(2)
    lane = lax.broadcasted_iota(jnp.int32, (t, V_DIM), 1)
    qmaps = []
    for hh in range(hps):
        q = q_ref[:, hh * V_DIM:(hh + 1) * V_DIM]
        zero = jnp.zeros_like(q)
        qmaps += [jnp.where(lane < HEAD_DIM, q, zero), jnp.where(lane >= HEAD_DIM, q, zero)]

    def rows(ki):
        return pl.ds(pl.multiple_of(ki * t, t), t)

    def logits(ki, hh, mi, which):
        k = k_ref[rows(ki), hh * V_DIM:(hh + 1) * V_DIM]
        s = lax.dot_general(k, qmaps[2 * hh + mi], (((1,), (1,)), ((), ())),
                            preferred_element_type=F32)
        return s if which is None else s + bias_ref[hh, which]

    def pv(ki, hh, p):
        v = v_ref[rows(ki), hh * V_DIM:(hh + 1) * V_DIM]
        return lax.dot_general(v, p.astype(BF16), (((0,), (0,)), ((), ())),
                               preferred_element_type=F32)

    def fold(ki, hh, mi, s, first):
        u = 2 * hh + mi
        if first:
            r_ref[u] = jnp.max(s, axis=0, keepdims=True)
        p = jnp.exp2(s - r_ref[u])
        lsum = jnp.sum(p, axis=0, keepdims=True)
        if first:
            l_ref[u] = lsum
            acc_ref[u] = pv(ki, hh, p)
        else:
            l_ref[u] += lsum
            acc_ref[u] += pv(ki, hh, p)

    def blocks(kis, which, first=False):
        pending = None
        for ki in kis:
            for hh in range(hps):
                for mi in range(2):
                    s = logits(ki, hh, mi, which)
                    if pending is not None:
                        fold(*pending, first)
                    pending = (ki, hh, mi, s)
        fold(*pending, first)

    n_far = jnp.maximum(qi - 1, 0)
    blocks([qi], 0, first=True)

    @pl.when(qi >= 1)
    def _():
        blocks([qi - 1], 1)

    def far_group(j, carry):
        blocks([ATTN_FAR_GROUP * j + g for g in range(ATTN_FAR_GROUP)], None)
        return carry

    lax.fori_loop(0, n_far // ATTN_FAR_GROUP, far_group, 0)
    g = ATTN_FAR_GROUP // 2
    while g >= 1:
        @pl.when((n_far & g) != 0)
        def _(g=g):
            base = n_far - (n_far & (2 * g - 1))
            blocks([base + i for i in range(g)], None)
        g //= 2

    flag = jnp.zeros((1, t), F32)
    for u in range(2 * hps):
        flag = jnp.maximum(flag, jnp.where(l_ref[u] < F32_HUGE, 0.0, 1.0))
        flag = jnp.maximum(flag, jnp.max(jnp.where(jnp.abs(acc_ref[u]) < F32_HUGE, 0.0, 1.0),
                                         axis=0, keepdims=True))
    overflowed = jnp.max(flag, axis=1, keepdims=True)[0, 0] > 0.0

    @pl.when(overflowed)
    def _():
        r_ref[...] = jnp.full(r_ref.shape, NEG_BIG, F32)
        l_ref[...] = jnp.zeros(l_ref.shape, F32)
        acc_ref[...] = jnp.zeros(acc_ref.shape, F32)

        def exact(ki, which):
            for hh in range(hps):
                for mi in range(2):
                    u = 2 * hh + mi
                    s = logits(ki, hh, mi, which)
                    m_old = r_ref[u]
                    m_new = jnp.maximum(m_old, jnp.max(s, axis=0, keepdims=True))
                    alpha = jnp.exp2(m_old - m_new)
                    p = jnp.exp2(s - m_new)
                    l_ref[u] = alpha * l_ref[u] + jnp.sum(p, axis=0, keepdims=True)
                    acc_ref[u] = alpha * acc_ref[u] + pv(ki, hh, p)
                    r_ref[u] = m_new

        exact(qi, 0)

        @pl.when(qi >= 1)
        def _():
            exact(qi - 1, 1)

        def far_exact(j, carry):
            exact(j, None)
            return carry

        lax.fori_loop(0, n_far, far_exact, 0)

    lam = (jnp.exp(jnp.sum(lam_ref[0:1, :] * lam_ref[1:2, :], axis=-1, keepdims=True))
           - jnp.exp(jnp.sum(lam_ref[2:3, :] * lam_ref[3:4, :], axis=-1, keepdims=True))
           + lam_init)
    for hh in range(hps):
        u = 2 * hh
        o_t = acc_ref[u] / l_ref[u] - lam * (acc_ref[u + 1] / l_ref[u + 1])
        o = o_t.T
        o_ref[:, hh * V_DIM:(hh + 1) * V_DIM] = (
            _rms(o, sg_ref[...]) * (1.0 - lam_init)).astype(BF16)


def _attention(z, bias, lam_vecs, subln_g, batch, seq, t, hps, lam_init):
    n = z.shape[0]
    nq = seq // t
    w = hps * V_DIM
    kern = functools.partial(_attn_kernel, t=t, hps=hps, lam_init=lam_init)
    return pl.pallas_call(
        kern,
        grid=(batch, N_HEADS // hps, nq),
        in_specs=[
            pl.BlockSpec((t, w), lambda b, h, i: (b * nq + i, COL_Q // w + h)),
            pl.BlockSpec((seq, w), lambda b, h, i: (b, COL_K // w + h)),
            pl.BlockSpec((seq, w), lambda b, h, i: (b, COL_V // w + h)),
            pl.BlockSpec((hps, 2, t, t), lambda b, h, i: (h, 0, 0, 0)),
            pl.BlockSpec((4, HEAD_DIM), lambda b, h, i: (0, 0)),
            pl.BlockSpec((1, V_DIM), lambda b, h, i: (0, 0)),
        ],
        out_specs=pl.BlockSpec((t, w), lambda b, h, i: (b * nq + i, h)),
        out_shape=jax.ShapeDtypeStruct((n, D_MODEL), BF16),
        scratch_shapes=[pltpu.VMEM((2 * hps, 1, t), F32),
                        pltpu.VMEM((2 * hps, 1, t), F32),
                        pltpu.VMEM((2 * hps, V_DIM, t), F32)],
        compiler_params=_cparams(("arbitrary", "arbitrary", "arbitrary")),
        name="diff_attn",
    )(z, z, z, bias, lam_vecs, subln_g)


def _t5_buckets(n):
    rel = np.arange(n)
    max_exact = N_BUCKETS // 2
    nf = np.maximum(rel, 1).astype(np.float32)
    large = max_exact + (np.log(nf / np.float32(max_exact)) / np.float32(math.log(MAX_DIST / max_exact))
                         * np.float32(N_BUCKETS - max_exact)).astype(np.int32)
    large = np.minimum(large, N_BUCKETS - 1)
    return np.where(rel < max_exact, rel, large)


def _bias_kernel(rb_ref, idx_ref, o_ref, *, far_bucket):
    h = pl.program_id(0)
    far = rb_ref[far_bucket, h]
    for which in range(2):
        idx = idx_ref[which]
        acc = jnp.full(idx.shape, NEG_BIG, F32)
        for b in range(N_BUCKETS):
            acc = jnp.where(idx == b, (rb_ref[b, h] - far) * LOG2E, acc)
        o_ref[which] = acc


def _bias_tiles(rel_bias, seq, t):
    buckets = _t5_buckets(seq)
    assert np.all(buckets[t + 1:] == buckets[-1])
    jj, ii = np.meshgrid(np.arange(t), np.arange(t), indexing="ij")
    rel_d = ii - jj
    idx_d = np.where(rel_d >= 0, buckets[np.maximum(rel_d, 0)], N_BUCKETS)
    idx_s = buckets[rel_d + t]
    idx = jnp.asarray(np.stack([idx_d, idx_s]).astype(np.int32))
    kern = functools.partial(_bias_kernel, far_bucket=int(buckets[-1]))
    return pl.pallas_call(
        kern,
        grid=(N_HEADS,),
        in_specs=[pl.BlockSpec(memory_space=pltpu.SMEM),
                  pl.BlockSpec((2, t, t), lambda h: (0, 0, 0))],
        out_specs=pl.BlockSpec((None, 2, t, t), lambda h: (h, 0, 0, 0)),
        out_shape=jax.ShapeDtypeStruct((N_HEADS, 2, t, t), F32),
        compiler_params=_cparams(("arbitrary",)),
        name="bias_tiles",
    )(rel_bias.astype(F32), idx)


def _pool_kernel(p_ref, ph_ref, w_ref, sc_ref, o_ref, buf_ref, *, tm, tiles_per_seq):
    pos = pl.program_id(0) % tiles_per_seq
    first = pos == 0
    buf_ref[POOL_HALO:POOL_HALO + tm, :] = p_ref[...].astype(F32)
    buf_ref[0:POOL_HALO, :] = jnp.where(first, 0.0, ph_ref[...].astype(F32))
    t_idx = pos * tm + lax.broadcasted_iota(jnp.int32, (tm, POOL_GW), 0)
    for g, win in enumerate(POOL_WINDOWS):
        cols = slice(g * POOL_GW, (g + 1) * POOL_GW)
        ext = buf_ref[:, cols]
        cur = ext[POOL_HALO:, :]
        tot, span = ext, 1
        while span < win:
            tot = tot + pltpu.roll(tot, span, 0)
            span *= 2
        tot = tot[POOL_HALO:, :]
        cnt = jnp.minimum(t_idx + 1, win).astype(F32)
        d = (tot / cnt - cur).astype(BF16)
        y = jnp.dot(d, w_ref[g], preferred_element_type=F32)
        o_ref[:, cols] = (y * sc_ref[:, cols]).astype(BF16)


def _pool_branch(z, w, scale, seq, tm):
    n = z.shape[0]
    hb = tm // POOL_HALO
    pc = COL_POOL // D_MODEL
    kern = functools.partial(_pool_kernel, tm=tm, tiles_per_seq=seq // tm)
    return pl.pallas_call(
        kern,
        grid=(n // tm,),
        in_specs=[
            pl.BlockSpec((tm, D_MODEL), lambda i: (i, pc)),
            pl.BlockSpec((POOL_HALO, D_MODEL), lambda i: (jnp.maximum(i * hb - 1, 0), pc)),
            pl.BlockSpec((len(POOL_WINDOWS), POOL_GW, POOL_GW), lambda i: (0, 0, 0)),
            pl.BlockSpec((1, D_MODEL), lambda i: (0, 0)),
        ],
        out_specs=pl.BlockSpec((tm, D_MODEL), lambda i: (i, 0)),
        out_shape=jax.ShapeDtypeStruct((n, D_MODEL), BF16),
        scratch_shapes=[pltpu.VMEM((tm + POOL_HALO, D_MODEL), F32)],
        compiler_params=_cparams(("arbitrary",)),
        name="pool_branch",
    )(z, z, w, scale)


def _merge_kernel(bc_ref, ba_ref, bp_ref, gt_ref, x_ref, wc_ref, wa_ref, wp_ref, wo_ref, o_ref):
    def gate(b):
        return jax.nn.sigmoid(gt_ref[:, b * D_MODEL:(b + 1) * D_MODEL].astype(F32))

    m = gate(0) * jnp.dot(bc_ref[...], wc_ref[...], preferred_element_type=F32)
    m = m + gate(1) * jnp.dot(ba_ref[...], wa_ref[...], preferred_element_type=F32)
    m = m + gate(2) * jnp.dot(bp_ref[...], wp_ref[...], preferred_element_type=F32)
    o_ref[...] = x_ref[...] + jnp.dot(m.astype(BF16), wo_ref[...], preferred_element_type=F32)


def _merge(b_conv, b_attn, b_pool, z, x2, wc, wa, wp, wo, tm):
    n = x2.shape[0]
    tile = pl.BlockSpec((tm, D_MODEL), lambda i: (i, 0))
    wspec = pl.BlockSpec((D_MODEL, D_MODEL), lambda i: (0, 0))
    return pl.pallas_call(
        _merge_kernel,
        grid=(n // tm,),
        in_specs=[tile, tile, tile,
                  pl.BlockSpec((tm, 3 * D_MODEL), lambda i: (i, COL_GATE // (3 * D_MODEL))),
                  tile, wspec, wspec, wspec, wspec],
        out_specs=tile,
        out_shape=jax.ShapeDtypeStruct((n, D_MODEL), F32),
        compiler_params=_cparams(("arbitrary",)),
        name="merge",
    )(b_conv, b_attn, b_pool, z, x2, wc, wa, wp, wo)


def _ffn_kernel(h_ref, g_ref, wu_ref, dw_ref, wd_ref, fg_ref, o_ref,
                xn_ref, carry_ref, acc_ref, *, tm, tiles_per_seq, final_norm):
    first = (pl.program_id(0) % tiles_per_seq) == 0

    @pl.when(first)
    def _():
        carry_ref[...] = jnp.zeros(carry_ref.shape, F32)

    h = h_ref[...]
    xn_ref[...] = _rms(h, g_ref[...]).astype(BF16)
    acc_ref[...] = jnp.zeros(acc_ref.shape, F32)

    def cols_of(c):
        return (slice(c * FFN_CHUNK, (c + 1) * FFN_CHUNK),
                slice(FFN_W + c * FFN_CHUNK, FFN_W + (c + 1) * FFN_CHUNK))

    def up(c):
        return [jnp.dot(xn_ref[...], wu_ref[:, cols], preferred_element_type=F32)
                for cols in cols_of(c)]

    def conv3(u, cols):
        ext = jnp.concatenate([carry_ref[:, cols], u], axis=0)
        carry_ref[:, cols] = u[tm - SUBLANES:, :]
        y = dw_ref[2:3, cols] * u
        for back in range(1, FFN_K):
            shifted = pltpu.roll(ext, back, 0)[SUBLANES:, :]
            y = y + dw_ref[FFN_K - 1 - back:FFN_K - back, cols] * shifted
        return y

    n_chunks = FFN_W // FFN_CHUNK
    ahead = 2
    pending = [up(c) for c in range(ahead)]
    for c in range(n_chunks):
        if c + ahead < n_chunks:
            pending.append(up(c + ahead))
        a, g = (conv3(u, cols) for u, cols in zip(pending.pop(0), cols_of(c)))
        mid = (g * jax.nn.sigmoid(g) * a).astype(BF16)
        acc_ref[...] += jnp.dot(mid, wd_ref[c * FFN_CHUNK:(c + 1) * FFN_CHUNK, :],
                                preferred_element_type=F32)

    out = h + acc_ref[...]
    if final_norm:
        out = _rms(out, fg_ref[...])
    o_ref[...] = out


def _ffn(h2, g, w_up, w_dw, w_down, final_g, seq, tm, final_norm):
    n = h2.shape[0]
    kern = functools.partial(_ffn_kernel, tm=tm, tiles_per_seq=seq // tm, final_norm=final_norm)
    tile = pl.BlockSpec((tm, D_MODEL), lambda i: (i, 0))
    vec = pl.BlockSpec((1, D_MODEL), lambda i: (0, 0))
    return pl.pallas_call(
        kern,
        grid=(n // tm,),
        in_specs=[tile, vec,
                  pl.BlockSpec((D_MODEL, 2 * FFN_W), lambda i: (0, 0)),
                  pl.BlockSpec((FFN_K, 2 * FFN_W), lambda i: (0, 0)),
                  pl.BlockSpec((FFN_W, D_MODEL), lambda i: (0, 0)),
                  vec],
        out_specs=tile,
        out_shape=jax.ShapeDtypeStruct((n, D_MODEL), F32),
        scratch_shapes=[pltpu.VMEM((tm, D_MODEL), BF16),
                        pltpu.VMEM((SUBLANES, 2 * FFN_W), F32),
                        pltpu.VMEM((tm, D_MODEL), F32)],
        compiler_params=_cparams(("arbitrary",)),
        name="conv_ffn",
    )(h2, g, w_up, w_dw, w_down, final_g)


def _tiles(seq):
    return dict(
        inproj_m=min(1024, seq), inproj_n=3 * D_MODEL,
        conv_m=min(256, seq), pool_m=min(512, seq),
        attn_t=min(512, seq), attn_heads=2, merge_m=min(512, seq), ffn_m=min(256, seq),
    )


def kernel(x, rel_bias, norm1_g, w_in, conv_dw_w, conv_dw_b, conv_ln_g, conv_ln_b, lam_q1, lam_k1, lam_q2, lam_k2, subln_g, pool_w, pool_scale, w_br_conv, w_br_attn, w_br_pool, w_o, norm2_g, ffn_up, ffn_dw, ffn_down, final_g):
    batch, seq, d = x.shape
    depth = w_in.shape[0]
    assert d == D_MODEL and w_in.shape[2] == IN_COLS
    ts = _tiles(seq)
    assert seq % ts["attn_t"] == 0 and ts["attn_t"] >= LANES

    colscale = np.ones((1, IN_COLS), np.float32)
    colscale[:, COL_Q:COL_K] = HEAD_DIM ** -0.5 * LOG2E
    colscale = jnp.asarray(colscale)
    bias = _bias_tiles(rel_bias, seq, ts["attn_t"])
    row = lambda v: v.astype(F32).reshape(1, -1)

    x2 = x.reshape(batch * seq, d)
    for l in range(depth):
        lam_init = 0.8 - 0.6 * math.exp(-0.3 * l)
        z = _inproj(x2, row(norm1_g[l]), w_in[l].astype(BF16), colscale,
                    ts["inproj_m"], ts["inproj_n"])
        b_conv = _conv_branch(z, conv_dw_w[l].astype(F32), row(conv_dw_b[l]),
                              row(conv_ln_g[l]), row(conv_ln_b[l]), seq, ts["conv_m"])
        lam_vecs = jnp.stack([lam_q1[l], lam_k1[l], lam_q2[l], lam_k2[l]]).astype(F32)
        b_attn = _attention(z, bias, lam_vecs, row(subln_g[l]), batch, seq, ts["attn_t"],
                            ts["attn_heads"], lam_init)
        b_pool = _pool_branch(z, pool_w[l].astype(BF16), row(pool_scale[l]), seq, ts["pool_m"])
        h2 = _merge(b_conv, b_attn, b_pool, z, x2,
                    w_br_conv[l].astype(BF16), w_br_attn[l].astype(BF16),
                    w_br_pool[l].astype(BF16), w_o[l].astype(BF16), ts["merge_m"])
        x2 = _ffn(h2, row(norm2_g[l]), ffn_up[l].astype(BF16), ffn_dw[l].astype(F32),
                  ffn_down[l].astype(BF16), row(final_g), seq, ts["ffn_m"],
                  final_norm=(l == depth - 1))
    return x2.reshape(batch, seq, d)
```

```python
import functools
import math

import jax
import jax.numpy as jnp
import numpy as np
from jax import lax
from jax.experimental import pallas as pl
from jax.experimental.pallas import tpu as pltpu

F32 = jnp.float32
BF16 = jnp.bfloat16

D_MODEL = 1024
N_HEADS = 8
HEAD_DIM = 64
V_DIM = 2 * HEAD_DIM
CONV_K = 31
POOL_WINDOWS = (2, 4, 8, 16)
POOL_GW = D_MODEL // len(POOL_WINDOWS)
FFN_W = 2816
FFN_K = 3
N_BUCKETS = 32
MAX_DIST = 128
EPS = 1e-6

COL_CONV = 0
COL_Q = 2 * D_MODEL
COL_K = COL_Q + D_MODEL
COL_V = COL_K + D_MODEL
COL_POOL = COL_V + D_MODEL
COL_GATE = COL_POOL + D_MODEL
IN_COLS = COL_GATE + 3 * D_MODEL

LANES = 128
SUBLANES = 8
VMEM_LIMIT = 56 * 1024 * 1024
NEG_BIG = -1e30
F32_HUGE = 3.0e38
SOFTMAX_SUM_MIN = 1e-30
SOFTMAX_SUM_MAX = 1e30
ATTN_FAR_GROUP = 4
LOG2E = 1.4426950408889634
CONV_HALO = 32
POOL_HALO = 16
FFN_CHUNK = 256

def _cparams(sem):
    return pltpu.CompilerParams(dimension_semantics=sem, vmem_limit_bytes=VMEM_LIMIT)


def _rms(x, g):
    ms = jnp.mean(x * x, axis=-1, keepdims=True)
    return x * lax.rsqrt(ms + EPS) * g


def _inproj_kernel(x_ref, g_ref, w_ref, cs_ref, o_ref, xn_ref):
    @pl.when(pl.program_id(1) == 0)
    def _():
        xn_ref[...] = _rms(x_ref[...], g_ref[...]).astype(BF16)

    for c in range(w_ref.shape[1] // D_MODEL):
        cols = slice(c * D_MODEL, (c + 1) * D_MODEL)
        acc = jnp.dot(xn_ref[...], w_ref[:, cols], preferred_element_type=F32)
        o_ref[:, cols] = (acc * cs_ref[:, cols]).astype(BF16)


def _inproj(x2, g, w, colscale, tm, tn):
    n = x2.shape[0]
    return pl.pallas_call(
        _inproj_kernel,
        grid=(n // tm, IN_COLS // tn),
        in_specs=[
            pl.BlockSpec((tm, D_MODEL), lambda i, j: (i, 0)),
            pl.BlockSpec((1, D_MODEL), lambda i, j: (0, 0)),
            pl.BlockSpec((D_MODEL, tn), lambda i, j: (0, j)),
            pl.BlockSpec((1, tn), lambda i, j: (0, j)),
        ],
        out_specs=pl.BlockSpec((tm, tn), lambda i, j: (i, j)),
        out_shape=jax.ShapeDtypeStruct((n, IN_COLS), BF16),
        scratch_shapes=[pltpu.VMEM((tm, D_MODEL), BF16)],
        compiler_params=_cparams(("arbitrary", "arbitrary")),
        name="inproj",
    )(x2, g, w, colscale)


def _conv_kernel(a_ref, g_ref, ah_ref, gh_ref, w_ref, b_ref, lg_ref, lb_ref, o_ref,
                 h_ref, y_ref, *, tm, tiles_per_seq, rows):
    first = (pl.program_id(0) % tiles_per_seq) == 0
    n = tm + CONV_HALO
    main = a_ref[...].astype(F32) * jax.nn.sigmoid(g_ref[...].astype(F32))
    halo = ah_ref[...].astype(F32) * jax.nn.sigmoid(gh_ref[...].astype(F32))
    h = jnp.concatenate([jnp.where(first, 0.0, halo), main], axis=0)
    h_ref[0] = h
    for b in range(1, SUBLANES):
        h_ref[b] = pltpu.roll(h, n - b, 0)

    lead = CONV_HALO - (CONV_K - 1)

    def chunk(r, carry):
        r0 = pl.multiple_of(r * rows, rows)
        for cg in range(D_MODEL // LANES):
            cols = slice(cg * LANES, (cg + 1) * LANES)
            acc = jnp.broadcast_to(b_ref[:, cols], (rows, LANES))
            for b in range(SUBLANES):
                hb = h_ref[b, pl.ds(r0, rows + CONV_HALO), cols]
                for off in range(b, CONV_HALO + 1, SUBLANES):
                    j = off - lead
                    if 0 <= j < CONV_K:
                        acc = acc + w_ref[j:j + 1, cols] * hb[off - b:off - b + rows]
            y_ref[pl.ds(r0, rows), cols] = acc
        return carry

    lax.fori_loop(0, tm // rows, chunk, 0)

    y = y_ref[...]
    mu = jnp.mean(y, axis=-1, keepdims=True)
    yc = y - mu
    var = jnp.mean(yc * yc, axis=-1, keepdims=True)
    yn = yc * lax.rsqrt(var + EPS) * lg_ref[...] + lb_ref[...]
    o_ref[...] = (yn * jax.nn.sigmoid(yn)).astype(BF16)


def _conv_branch(z, w, b, lg, lb, seq, tm):
    n = z.shape[0]
    hb = tm // CONV_HALO
    halo_idx = lambda i: jnp.maximum(i * hb - 1, 0)
    kern = functools.partial(_conv_kernel, tm=tm, tiles_per_seq=seq // tm, rows=32)
    vec = pl.BlockSpec((1, D_MODEL), lambda i: (0, 0))
    return pl.pallas_call(
        kern,
        grid=(n // tm,),
        in_specs=[
            pl.BlockSpec((tm, D_MODEL), lambda i: (i, 0)),
            pl.BlockSpec((tm, D_MODEL), lambda i: (i, 1)),
            pl.BlockSpec((CONV_HALO, D_MODEL), lambda i: (halo_idx(i), 0)),
            pl.BlockSpec((CONV_HALO, D_MODEL), lambda i: (halo_idx(i), 1)),
            pl.BlockSpec((CONV_K, D_MODEL), lambda i: (0, 0)),
            vec, vec, vec,
        ],
        out_specs=pl.BlockSpec((tm, D_MODEL), lambda i: (i, 0)),
        out_shape=jax.ShapeDtypeStruct((n, D_MODEL), BF16),
        scratch_shapes=[pltpu.VMEM((SUBLANES, tm + CONV_HALO, D_MODEL), F32),
                        pltpu.VMEM((tm, D_MODEL), F32)],
        compiler_params=_cparams(("arbitrary",)),
        name="conv_branch",
    )(z, z, z, z, w, b, lg, lb)


def _attn_kernel(q_ref, k_ref, v_ref, bias_ref, lam_ref, sg_ref, o_ref,
                 r_ref, l_ref, acc_ref, *, t, hps, lam_init):
    qi = pl.program_id(2)
    lane = lax.broadcasted_iota(jnp.int32, (t, V_DIM), 1)
    qmaps = []
    for hh in range(hps):
        q = q_ref[:, hh * V_DIM:(hh + 1) * V_DIM]
        zero = jnp.zeros_like(q)
        qmaps += [jnp.where(lane < HEAD_DIM, q, zero), jnp.where(lane >= HEAD_DIM, q, zero)]

    def rows(ki):
        return pl.ds(pl.multiple_of(ki * t, t), t)

    def logits(ki, hh, mi, which):
        k = k_ref[rows(ki), hh * V_DIM:(hh + 1) * V_DIM]
        s = lax.dot_general(k, qmaps[2 * hh + mi], (((1,), (1,)), ((), ())),
                            preferred_element_type=F32)
        return s if which is None else s + bias_ref[hh, which]

    def pv(ki, hh, p):
        v = v_ref[rows(ki), hh * V_DIM:(hh + 1) * V_DIM]
        return lax.dot_general(v, p.astype(BF16), (((0,), (0,)), ((), ())),
                               preferred_element_type=F32)

    def fold(ki, hh, mi, s, init):
        u = 2 * hh + mi
        p = jnp.exp2(s)
        lsum = jnp.sum(p, axis=0, keepdims=True)
        if init:
            l_ref[u] = lsum
            acc_ref[u] = pv(ki, hh, p)
        else:
            l_ref[u] += lsum
            acc_ref[u] += pv(ki, hh, p)

    def blocks(specs):
        pending = None
        for ki, which, init in specs:
            for hh in range(hps):
                for mi in range(2):
                    s = logits(ki, hh, mi, which)
                    if pending is not None:
                        fold(*pending)
                    pending = (ki, hh, mi, s, init)
        fold(*pending)

    n_far = jnp.maximum(qi - 1, 0)

    @pl.when(qi == 0)
    def _():
        blocks([(qi, 0, True)])

    @pl.when(qi >= 1)
    def _():
        blocks([(qi, 0, True), (qi - 1, 1, False)])

    def far_group(j, carry):
        blocks([(ATTN_FAR_GROUP * j + g, None, False) for g in range(ATTN_FAR_GROUP)])
        return carry

    lax.fori_loop(0, n_far // ATTN_FAR_GROUP, far_group, 0)
    g = ATTN_FAR_GROUP // 2
    while g >= 1:
        @pl.when((n_far & g) != 0)
        def _(g=g):
            base = n_far - (n_far & (2 * g - 1))
            blocks([(base + i, None, False) for i in range(g)])
        g //= 2

    flag = jnp.zeros((1, t), F32)
    for u in range(2 * hps):
        l = l_ref[u]
        in_range = jnp.logical_and(l > SOFTMAX_SUM_MIN, l < SOFTMAX_SUM_MAX)
        flag = jnp.maximum(flag, jnp.where(in_range, 0.0, 1.0))
        amax = jnp.max(jnp.abs(acc_ref[u]), axis=0, keepdims=True)
        flag = jnp.maximum(flag, jnp.where(amax < F32_HUGE, 0.0, 1.0))
    overflowed = jnp.max(flag, axis=1, keepdims=True)[0, 0] > 0.0

    @pl.when(overflowed)
    def _():
        r_ref[...] = jnp.full(r_ref.shape, NEG_BIG, F32)
        l_ref[...] = jnp.zeros(l_ref.shape, F32)
        acc_ref[...] = jnp.zeros(acc_ref.shape, F32)

        def exact(ki, which):
            for hh in range(hps):
                for mi in range(2):
                    u = 2 * hh + mi
                    s = logits(ki, hh, mi, which)
                    m_old = r_ref[u]
                    m_new = jnp.maximum(m_old, jnp.max(s, axis=0, keepdims=True))
                    alpha = jnp.exp2(m_old - m_new)
                    p = jnp.exp2(s - m_new)
                    l_ref[u] = alpha * l_ref[u] + jnp.sum(p, axis=0, keepdims=True)
                    acc_ref[u] = alpha * acc_ref[u] + pv(ki, hh, p)
                    r_ref[u] = m_new

        exact(qi, 0)

        @pl.when(qi >= 1)
        def _():
            exact(qi - 1, 1)

        def far_exact(j, carry):
            exact(j, None)
            return carry

        lax.fori_loop(0, n_far, far_exact, 0)

    lam = (jnp.exp(jnp.sum(lam_ref[0:1, :] * lam_ref[1:2, :], axis=-1, keepdims=True))
           - jnp.exp(jnp.sum(lam_ref[2:3, :] * lam_ref[3:4, :], axis=-1, keepdims=True))
           + lam_init)
    for hh in range(hps):
        u = 2 * hh
        o_t = (acc_ref[u] * (1.0 / l_ref[u])
               - acc_ref[u + 1] * (lam / l_ref[u + 1]))
        o = o_t.T
        o_ref[:, hh * V_DIM:(hh + 1) * V_DIM] = (
            _rms(o, sg_ref[...]) * (1.0 - lam_init)).astype(BF16)


def _attention(z, bias, lam_vecs, subln_g, batch, seq, t, hps, lam_init):
    n = z.shape[0]
    nq = seq // t
    w = hps * V_DIM
    kern = functools.partial(_attn_kernel, t=t, hps=hps, lam_init=lam_init)
    return pl.pallas_call(
        kern,
        grid=(batch, N_HEADS // hps, nq),
        in_specs=[
            pl.BlockSpec((t, w), lambda b, h, i: (b * nq + i, COL_Q // w + h)),
            pl.BlockSpec((seq, w), lambda b, h, i: (b, COL_K // w + h)),
            pl.BlockSpec((seq, w), lambda b, h, i: (b, COL_V // w + h)),
            pl.BlockSpec((hps, 2, t, t), lambda b, h, i: (h, 0, 0, 0)),
            pl.BlockSpec((4, HEAD_DIM), lambda b, h, i: (0, 0)),
            pl.BlockSpec((1, V_DIM), lambda b, h, i: (0, 0)),
        ],
        out_specs=pl.BlockSpec((t, w), lambda b, h, i: (b * nq + i, h)),
        out_shape=jax.ShapeDtypeStruct((n, D_MODEL), BF16),
        scratch_shapes=[pltpu.VMEM((2 * hps, 1, t), F32),
                        pltpu.VMEM((2 * hps, 1, t), F32),
                        pltpu.VMEM((2 * hps, V_DIM, t), F32)],
        compiler_params=_cparams(("arbitrary", "arbitrary", "arbitrary")),
        name="diff_attn",
    )(z, z, z, bias, lam_vecs, subln_g)


def _t5_buckets(n):
    rel = np.arange(n)
    max_exact = N_BUCKETS // 2
    nf = np.maximum(rel, 1).astype(np.float32)
    large = max_exact + (np.log(nf / np.float32(max_exact)) / np.float32(math.log(MAX_DIST / max_exact))
                         * np.float32(N_BUCKETS - max_exact)).astype(np.int32)
    large = np.minimum(large, N_BUCKETS - 1)
    return np.where(rel < max_exact, rel, large)


def _bias_kernel(rb_ref, idx_ref, o_ref, *, far_bucket):
    h = pl.program_id(0)
    far = rb_ref[far_bucket, h]
    for which in range(2):
        idx = idx_ref[which]
        acc = jnp.full(idx.shape, NEG_BIG, F32)
        for b in range(N_BUCKETS):
            acc = jnp.where(idx == b, (rb_ref[b, h] - far) * LOG2E, acc)
        o_ref[which] = acc


def _bias_tiles(rel_bias, seq, t):
    buckets = _t5_buckets(seq)
    assert np.all(buckets[t + 1:] == buckets[-1])
    jj, ii = np.meshgrid(np.arange(t), np.arange(t), indexing="ij")
    rel_d = ii - jj
    idx_d = np.where(rel_d >= 0, buckets[np.maximum(rel_d, 0)], N_BUCKETS)
    idx_s = buckets[rel_d + t]
    idx = jnp.asarray(np.stack([idx_d, idx_s]).astype(np.int32))
    kern = functools.partial(_bias_kernel, far_bucket=int(buckets[-1]))
    return pl.pallas_call(
        kern,
        grid=(N_HEADS,),
        in_specs=[pl.BlockSpec(memory_space=pltpu.SMEM),
                  pl.BlockSpec((2, t, t), lambda h: (0, 0, 0))],
        out_specs=pl.BlockSpec((None, 2, t, t), lambda h: (h, 0, 0, 0)),
        out_shape=jax.ShapeDtypeStruct((N_HEADS, 2, t, t), F32),
        compiler_params=_cparams(("arbitrary",)),
        name="bias_tiles",
    )(rel_bias.astype(F32), idx)


def _pool_kernel(p_ref, ph_ref, w_ref, sc_ref, o_ref, buf_ref, *, tm, tiles_per_seq):
    pos = pl.program_id(0) % tiles_per_seq
    first = pos == 0
    buf_ref[POOL_HALO:POOL_HALO + tm, :] = p_ref[...].astype(F32)
    buf_ref[0:POOL_HALO, :] = jnp.where(first, 0.0, ph_ref[...].astype(F32))
    t_idx = pos * tm + lax.broadcasted_iota(jnp.int32, (tm, POOL_GW), 0)
    for g, win in enumerate(POOL_WINDOWS):
        cols = slice(g * POOL_GW, (g + 1) * POOL_GW)
        ext = buf_ref[:, cols]
        cur = ext[POOL_HALO:, :]
        tot, span = ext, 1
        while span < win:
            tot = tot + pltpu.roll(tot, span, 0)
            span *= 2
        tot = tot[POOL_HALO:, :]
        cnt = jnp.minimum(t_idx + 1, win).astype(F32)
        d = (tot / cnt - cur).astype(BF16)
        y = jnp.dot(d, w_ref[g], preferred_element_type=F32)
        o_ref[:, cols] = (y * sc_ref[:, cols]).astype(BF16)


def _pool_branch(z, w, scale, seq, tm):
    n = z.shape[0]
    hb = tm // POOL_HALO
    pc = COL_POOL // D_MODEL
    kern = functools.partial(_pool_kernel, tm=tm, tiles_per_seq=seq // tm)
    return pl.pallas_call(
        kern,
        grid=(n // tm,),
        in_specs=[
            pl.BlockSpec((tm, D_MODEL), lambda i: (i, pc)),
            pl.BlockSpec((POOL_HALO, D_MODEL), lambda i: (jnp.maximum(i * hb - 1, 0), pc)),
            pl.BlockSpec((len(POOL_WINDOWS), POOL_GW, POOL_GW), lambda i: (0, 0, 0)),
            pl.BlockSpec((1, D_MODEL), lambda i: (0, 0)),
        ],
        out_specs=pl.BlockSpec((tm, D_MODEL), lambda i: (i, 0)),
        out_shape=jax.ShapeDtypeStruct((n, D_MODEL), BF16),
        scratch_shapes=[pltpu.VMEM((tm + POOL_HALO, D_MODEL), F32)],
        compiler_params=_cparams(("arbitrary",)),
        name="pool_branch",
    )(z, z, w, scale)


def _merge_kernel(bc_ref, ba_ref, bp_ref, gt_ref, x_ref, wc_ref, wa_ref, wp_ref, wo_ref, o_ref):
    def gate(b):
        return jax.nn.sigmoid(gt_ref[:, b * D_MODEL:(b + 1) * D_MODEL].astype(F32))

    m = gate(0) * jnp.dot(bc_ref[...], wc_ref[...], preferred_element_type=F32)
    m = m + gate(1) * jnp.dot(ba_ref[...], wa_ref[...], preferred_element_type=F32)
    m = m + gate(2) * jnp.dot(bp_ref[...], wp_ref[...], preferred_element_type=F32)
    o_ref[...] = x_ref[...] + jnp.dot(m.astype(BF16), wo_ref[...], preferred_element_type=F32)


def _merge(b_conv, b_attn, b_pool, z, x2, wc, wa, wp, wo, tm):
    n = x2.shape[0]
    tile = pl.BlockSpec((tm, D_MODEL), lambda i: (i, 0))
    wspec = pl.BlockSpec((D_MODEL, D_MODEL), lambda i: (0, 0))
    return pl.pallas_call(
        _merge_kernel,
        grid=(n // tm,),
        in_specs=[tile, tile, tile,
                  pl.BlockSpec((tm, 3 * D_MODEL), lambda i: (i, COL_GATE // (3 * D_MODEL))),
                  tile, wspec, wspec, wspec, wspec],
        out_specs=tile,
        out_shape=jax.ShapeDtypeStruct((n, D_MODEL), F32),
        compiler_params=_cparams(("arbitrary",)),
        name="merge",
    )(b_conv, b_attn, b_pool, z, x2, wc, wa, wp, wo)


def _ffn_kernel(h_ref, g_ref, wu_ref, dw_ref, wd_ref, fg_ref, o_ref,
                xn_ref, carry_ref, acc_ref, *, tm, tiles_per_seq, final_norm):
    first = (pl.program_id(0) % tiles_per_seq) == 0

    @pl.when(first)
    def _():
        carry_ref[...] = jnp.zeros(carry_ref.shape, F32)

    h = h_ref[...]
    xn_ref[...] = _rms(h, g_ref[...]).astype(BF16)
    acc_ref[...] = jnp.zeros(acc_ref.shape, F32)

    def cols_of(c):
        return (slice(c * FFN_CHUNK, (c + 1) * FFN_CHUNK),
                slice(FFN_W + c * FFN_CHUNK, FFN_W + (c + 1) * FFN_CHUNK))

    def up(c):
        return [jnp.dot(xn_ref[...], wu_ref[:, cols], preferred_element_type=F32)
                for cols in cols_of(c)]

    def conv3(u, cols):
        ext = jnp.concatenate([carry_ref[:, cols], u], axis=0)
        carry_ref[:, cols] = u[tm - SUBLANES:, :]
        y = dw_ref[2:3, cols] * u
        for back in range(1, FFN_K):
            shifted = pltpu.roll(ext, back, 0)[SUBLANES:, :]
            y = y + dw_ref[FFN_K - 1 - back:FFN_K - back, cols] * shifted
        return y

    n_chunks = FFN_W // FFN_CHUNK
    ahead = 3
    pending = [up(c) for c in range(ahead)]
    for c in range(n_chunks):
        if c + ahead < n_chunks:
            pending.append(up(c + ahead))
        a, g = (conv3(u, cols) for u, cols in zip(pending.pop(0), cols_of(c)))
        mid = (g * jax.nn.sigmoid(g) * a).astype(BF16)
        acc_ref[...] += jnp.dot(mid, wd_ref[c * FFN_CHUNK:(c + 1) * FFN_CHUNK, :],
                                preferred_element_type=F32)

    out = h + acc_ref[...]
    if final_norm:
        out = _rms(out, fg_ref[...])
    o_ref[...] = out


def _ffn(h2, g, w_up, w_dw, w_down, final_g, seq, tm, final_norm):
    n = h2.shape[0]
    kern = functools.partial(_ffn_kernel, tm=tm, tiles_per_seq=seq // tm, final_norm=final_norm)
    tile = pl.BlockSpec((tm, D_MODEL), lambda i: (i, 0))
    vec = pl.BlockSpec((1, D_MODEL), lambda i: (0, 0))
    return pl.pallas_call(
        kern,
        grid=(n // tm,),
        in_specs=[tile, vec,
                  pl.BlockSpec((D_MODEL, 2 * FFN_W), lambda i: (0, 0)),
                  pl.BlockSpec((FFN_K, 2 * FFN_W), lambda i: (0, 0)),
                  pl.BlockSpec((FFN_W, D_MODEL), lambda i: (0, 0)),
                  vec],
        out_specs=tile,
        out_shape=jax.ShapeDtypeStruct((n, D_MODEL), F32),
        scratch_shapes=[pltpu.VMEM((tm, D_MODEL), BF16),
                        pltpu.VMEM((SUBLANES, 2 * FFN_W), F32),
                        pltpu.VMEM((tm, D_MODEL), F32)],
        compiler_params=_cparams(("arbitrary",)),
        name="conv_ffn",
    )(h2, g, w_up, w_dw, w_down, final_g)


def _tiles(seq):
    return dict(
        inproj_m=min(1024, seq), inproj_n=3 * D_MODEL,
        conv_m=min(256, seq), pool_m=min(512, seq),
        attn_t=min(512, seq), attn_heads=2, merge_m=min(512, seq), ffn_m=min(256, seq),
    )


def kernel(x, rel_bias, norm1_g, w_in, conv_dw_w, conv_dw_b, conv_ln_g, conv_ln_b, lam_q1, lam_k1, lam_q2, lam_k2, subln_g, pool_w, pool_scale, w_br_conv, w_br_attn, w_br_pool, w_o, norm2_g, ffn_up, ffn_dw, ffn_down, final_g):
    batch, seq, d = x.shape
    depth = w_in.shape[0]
    assert d == D_MODEL and w_in.shape[2] == IN_COLS
    ts = _tiles(seq)
    assert seq % ts["attn_t"] == 0 and ts["attn_t"] >= LANES

    colscale = np.ones((1, IN_COLS), np.float32)
    colscale[:, COL_Q:COL_K] = HEAD_DIM ** -0.5 * LOG2E
    colscale = jnp.asarray(colscale)
    bias = _bias_tiles(rel_bias, seq, ts["attn_t"])
    row = lambda v: v.astype(F32).reshape(1, -1)

    x2 = x.reshape(batch * seq, d)
    for l in range(depth):
        lam_init = 0.8 - 0.6 * math.exp(-0.3 * l)
        z = _inproj(x2, row(norm1_g[l]), w_in[l].astype(BF16), colscale,
                    ts["inproj_m"], ts["inproj_n"])
        b_conv = _conv_branch(z, conv_dw_w[l].astype(F32), row(conv_dw_b[l]),
                              row(conv_ln_g[l]), row(conv_ln_b[l]), seq, ts["conv_m"])
        lam_vecs = jnp.stack([lam_q1[l], lam_k1[l], lam_q2[l], lam_k2[l]]).astype(F32)
        b_attn = _attention(z, bias, lam_vecs, row(subln_g[l]), batch, seq, ts["attn_t"],
                            ts["attn_heads"], lam_init)
        b_pool = _pool_branch(z, pool_w[l].astype(BF16), row(pool_scale[l]), seq, ts["pool_m"])
        h2 = _merge(b_conv, b_attn, b_pool, z, x2,
                    w_br_conv[l].astype(BF16), w_br_attn[l].astype(BF16),
                    w_br_pool[l].astype(BF16), w_o[l].astype(BF16), ts["merge_m"])
        x2 = _ffn(h2, row(norm2_g[l]), ffn_up[l].astype(BF16), ffn_dw[l].astype(F32),
                  ffn_down[l].astype(BF16), row(final_g), seq, ts["ffn_m"],
                  final_norm=(l == depth - 1))
    return x2.reshape(batch, seq, d)
```

```python
import functools
import math

import jax
import jax.numpy as jnp
import numpy as np
from jax import lax
from jax.experimental import pallas as pl
from jax.experimental.pallas import tpu as pltpu

F32 = jnp.float32
BF16 = jnp.bfloat16

D_MODEL = 1024
N_HEADS = 8
HEAD_DIM = 64
V_DIM = 2 * HEAD_DIM
CONV_K = 31
POOL_WINDOWS = (2, 4, 8, 16)
POOL_GW = D_MODEL // len(POOL_WINDOWS)
FFN_W = 2816
FFN_K = 3
N_BUCKETS = 32
MAX_DIST = 128
EPS = 1e-6

COL_CONV = 0
COL_Q = 2 * D_MODEL
COL_K = COL_Q + D_MODEL
COL_V = COL_K + D_MODEL
COL_POOL = COL_V + D_MODEL
COL_GATE = COL_POOL + D_MODEL
IN_COLS = COL_GATE + 3 * D_MODEL

LANES = 128
SUBLANES = 8
VMEM_LIMIT = 56 * 1024 * 1024
NEG_BIG = -1e30
F32_HUGE = 3.0e38
SOFTMAX_SUM_MIN = 1e-30
SOFTMAX_SUM_MAX = 1e30
ATTN_FAR_GROUP = 4
LOG2E = 1.4426950408889634
CONV_HALO = 32
POOL_HALO = 16
FFN_CHUNK = 256

def _cparams(sem):
    return pltpu.CompilerParams(dimension_semantics=sem, vmem_limit_bytes=VMEM_LIMIT)


def _rms(x, g):
    ms = jnp.mean(x * x, axis=-1, keepdims=True)
    return x * lax.rsqrt(ms + EPS) * g


def _inproj_kernel(x_ref, g_ref, w_ref, cs_ref, o_ref, xn_ref):
    @pl.when(pl.program_id(1) == 0)
    def _():
        xn_ref[...] = _rms(x_ref[...], g_ref[...]).astype(BF16)

    for c in range(w_ref.shape[1] // D_MODEL):
        cols = slice(c * D_MODEL, (c + 1) * D_MODEL)
        acc = jnp.dot(xn_ref[...], w_ref[:, cols], preferred_element_type=F32)
        o_ref[:, cols] = (acc * cs_ref[:, cols]).astype(BF16)


def _inproj(x2, g, w, colscale, tm, tn):
    n = x2.shape[0]
    return pl.pallas_call(
        _inproj_kernel,
        grid=(n // tm, IN_COLS // tn),
        in_specs=[
            pl.BlockSpec((tm, D_MODEL), lambda i, j: (i, 0)),
            pl.BlockSpec((1, D_MODEL), lambda i, j: (0, 0)),
            pl.BlockSpec((D_MODEL, tn), lambda i, j: (0, j)),
            pl.BlockSpec((1, tn), lambda i, j: (0, j)),
        ],
        out_specs=pl.BlockSpec((tm, tn), lambda i, j: (i, j)),
        out_shape=jax.ShapeDtypeStruct((n, IN_COLS), BF16),
        scratch_shapes=[pltpu.VMEM((tm, D_MODEL), BF16)],
        compiler_params=_cparams(("arbitrary", "arbitrary")),
        name="inproj",
    )(x2, g, w, colscale)


def _conv_kernel(a_ref, g_ref, ah_ref, gh_ref, w_ref, b_ref, lg_ref, lb_ref, o_ref,
                 h_ref, y_ref, *, tm, tiles_per_seq, rows):
    first = (pl.program_id(0) % tiles_per_seq) == 0
    n = tm + CONV_HALO
    main = a_ref[...].astype(F32) * jax.nn.sigmoid(g_ref[...].astype(F32))
    halo = ah_ref[...].astype(F32) * jax.nn.sigmoid(gh_ref[...].astype(F32))
    h = jnp.concatenate([jnp.where(first, 0.0, halo), main], axis=0)
    h_ref[0] = h
    for b in range(1, SUBLANES):
        h_ref[b] = pltpu.roll(h, n - b, 0)

    lead = CONV_HALO - (CONV_K - 1)

    def chunk(r, carry):
        r0 = pl.multiple_of(r * rows, rows)
        for cg in range(D_MODEL // LANES):
            cols = slice(cg * LANES, (cg + 1) * LANES)
            acc = jnp.broadcast_to(b_ref[:, cols], (rows, LANES))
            for b in range(SUBLANES):
                hb = h_ref[b, pl.ds(r0, rows + CONV_HALO), cols]
                for off in range(b, CONV_HALO + 1, SUBLANES):
                    j = off - lead
                    if 0 <= j < CONV_K:
                        acc = acc + w_ref[j:j + 1, cols] * hb[off - b:off - b + rows]
            y_ref[pl.ds(r0, rows), cols] = acc
        return carry

    lax.fori_loop(0, tm // rows, chunk, 0)

    y = y_ref[...]
    mu = jnp.mean(y, axis=-1, keepdims=True)
    yc = y - mu
    var = jnp.mean(yc * yc, axis=-1, keepdims=True)
    yn = yc * lax.rsqrt(var + EPS) * lg_ref[...] + lb_ref[...]
    o_ref[...] = (yn * jax.nn.sigmoid(yn)).astype(BF16)


def _conv_branch(z, w, b, lg, lb, seq, tm):
    n = z.shape[0]
    hb = tm // CONV_HALO
    halo_idx = lambda i: jnp.maximum(i * hb - 1, 0)
    kern = functools.partial(_conv_kernel, tm=tm, tiles_per_seq=seq // tm, rows=32)
    vec = pl.BlockSpec((1, D_MODEL), lambda i: (0, 0))
    return pl.pallas_call(
        kern,
        grid=(n // tm,),
        in_specs=[
            pl.BlockSpec((tm, D_MODEL), lambda i: (i, 0)),
            pl.BlockSpec((tm, D_MODEL), lambda i: (i, 1)),
            pl.BlockSpec((CONV_HALO, D_MODEL), lambda i: (halo_idx(i), 0)),
            pl.BlockSpec((CONV_HALO, D_MODEL), lambda i: (halo_idx(i), 1)),
            pl.BlockSpec((CONV_K, D_MODEL), lambda i: (0, 0)),
            vec, vec, vec,
        ],
        out_specs=pl.BlockSpec((tm, D_MODEL), lambda i: (i, 0)),
        out_shape=jax.ShapeDtypeStruct((n, D_MODEL), BF16),
        scratch_shapes=[pltpu.VMEM((SUBLANES, tm + CONV_HALO, D_MODEL), F32),
                        pltpu.VMEM((tm, D_MODEL), F32)],
        compiler_params=_cparams(("arbitrary",)),
        name="conv_branch",
    )(z, z, z, z, w, b, lg, lb)


def _attn_kernel(q_ref, k_ref, v_ref, bias_ref, lam_ref, sg_ref, o_ref,
                 r_ref, l_ref, acc_ref, *, t, hps, lam_init):
    qi = pl.program_id(2)
    lane = lax.broadcasted_iota(jnp.int32, (t, V_DIM), 1)
    qmaps = []
    for hh in range(hps):
        q = q_ref[:, hh * V_DIM:(hh + 1) * V_DIM]
        zero = jnp.zeros_like(q)
        qmaps += [jnp.where(lane < HEAD_DIM, q, zero), jnp.where(lane >= HEAD_DIM, q, zero)]

    def rows(ki):
        return pl.ds(pl.multiple_of(ki * t, t), t)

    def logits(ki, hh, mi, which):
        k = k_ref[rows(ki), hh * V_DIM:(hh + 1) * V_DIM]
        s = lax.dot_general(k, qmaps[2 * hh + mi], (((1,), (1,)), ((), ())),
                            preferred_element_type=F32)
        return s if which is None else s + bias_ref[hh, which]

    def pv(ki, hh, p):
        v = v_ref[rows(ki), hh * V_DIM:(hh + 1) * V_DIM]
        return lax.dot_general(v, p.astype(BF16), (((0,), (0,)), ((), ())),
                               preferred_element_type=F32)

    def fold(ki, hh, mi, s, init):
        u = 2 * hh + mi
        p = jnp.exp2(s)
        lsum = jnp.sum(p, axis=0, keepdims=True)
        if init:
            l_ref[u] = lsum
            acc_ref[u] = pv(ki, hh, p)
        else:
            l_ref[u] += lsum
            acc_ref[u] += pv(ki, hh, p)

    def blocks(specs):
        pending = None
        for ki, which, init in specs:
            for hh in range(hps):
                for mi in range(2):
                    s = logits(ki, hh, mi, which)
                    if pending is not None:
                        fold(*pending)
                    pending = (ki, hh, mi, s, init)
        fold(*pending)

    n_far = jnp.maximum(qi - 1, 0)

    @pl.when(qi == 0)
    def _():
        blocks([(qi, 0, True)])

    @pl.when(qi >= 1)
    def _():
        blocks([(qi, 0, True), (qi - 1, 1, False)])

    def far_group(j, carry):
        blocks([(ATTN_FAR_GROUP * j + g, None, False) for g in range(ATTN_FAR_GROUP)])
        return carry

    lax.fori_loop(0, n_far // ATTN_FAR_GROUP, far_group, 0)
    g = ATTN_FAR_GROUP // 2
    while g >= 1:
        @pl.when((n_far & g) != 0)
        def _(g=g):
            base = n_far - (n_far & (2 * g - 1))
            blocks([(base + i, None, False) for i in range(g)])
        g //= 2

    flag = jnp.zeros((1, t), F32)
    for u in range(2 * hps):
        l = l_ref[u]
        in_range = jnp.logical_and(l > SOFTMAX_SUM_MIN, l < SOFTMAX_SUM_MAX)
        flag = jnp.maximum(flag, jnp.where(in_range, 0.0, 1.0))
        amax = jnp.max(jnp.abs(acc_ref[u]), axis=0, keepdims=True)
        flag = jnp.maximum(flag, jnp.where(amax < F32_HUGE, 0.0, 1.0))
    overflowed = jnp.max(flag, axis=1, keepdims=True)[0, 0] > 0.0

    @pl.when(overflowed)
    def _():
        r_ref[...] = jnp.full(r_ref.shape, NEG_BIG, F32)
        l_ref[...] = jnp.zeros(l_ref.shape, F32)
        acc_ref[...] = jnp.zeros(acc_ref.shape, F32)

        def exact(ki, which):
            for hh in range(hps):
                for mi in range(2):
                    u = 2 * hh + mi
                    s = logits(ki, hh, mi, which)
                    m_old = r_ref[u]
                    m_new = jnp.maximum(m_old, jnp.max(s, axis=0, keepdims=True))
                    alpha = jnp.exp2(m_old - m_new)
                    p = jnp.exp2(s - m_new)
                    l_ref[u] = alpha * l_ref[u] + jnp.sum(p, axis=0, keepdims=True)
                    acc_ref[u] = alpha * acc_ref[u] + pv(ki, hh, p)
                    r_ref[u] = m_new

        exact(qi, 0)

        @pl.when(qi >= 1)
        def _():
            exact(qi - 1, 1)

        def far_exact(j, carry):
            exact(j, None)
            return carry

        lax.fori_loop(0, n_far, far_exact, 0)

    lam = (jnp.exp(jnp.sum(lam_ref[0:1, :] * lam_ref[1:2, :], axis=-1, keepdims=True))
           - jnp.exp(jnp.sum(lam_ref[2:3, :] * lam_ref[3:4, :], axis=-1, keepdims=True))
           + lam_init)
    for hh in range(hps):
        u = 2 * hh
        o_t = (acc_ref[u] * (1.0 / l_ref[u])
               - acc_ref[u + 1] * (lam / l_ref[u + 1]))
        o = o_t.T
        o_ref[:, hh * V_DIM:(hh + 1) * V_DIM] = (
            _rms(o, sg_ref[...]) * (1.0 - lam_init)).astype(BF16)


def _attention(z, bias, lam_vecs, subln_g, batch, seq, t, hps, lam_init):
    n = z.shape[0]
    nq = seq // t
    w = hps * V_DIM
    kern = functools.partial(_attn_kernel, t=t, hps=hps, lam_init=lam_init)
    return pl.pallas_call(
        kern,
        grid=(batch, N_HEADS // hps, nq),
        in_specs=[
            pl.BlockSpec((t, w), lambda b, h, i: (b * nq + i, COL_Q // w + h)),
            pl.BlockSpec((seq, w), lambda b, h, i: (b, COL_K // w + h)),
            pl.BlockSpec((seq, w), lambda b, h, i: (b, COL_V // w + h)),
            pl.BlockSpec((hps, 2, t, t), lambda b, h, i: (h, 0, 0, 0),
                         pipeline_mode=pl.Buffered(1)),
            pl.BlockSpec((4, HEAD_DIM), lambda b, h, i: (0, 0)),
            pl.BlockSpec((1, V_DIM), lambda b, h, i: (0, 0)),
        ],
        out_specs=pl.BlockSpec((t, w), lambda b, h, i: (b * nq + i, h)),
        out_shape=jax.ShapeDtypeStruct((n, D_MODEL), BF16),
        scratch_shapes=[pltpu.VMEM((2 * hps, 1, t), F32),
                        pltpu.VMEM((2 * hps, 1, t), F32),
                        pltpu.VMEM((2 * hps, V_DIM, t), F32)],
        compiler_params=_cparams(("arbitrary", "arbitrary", "arbitrary")),
        name="diff_attn",
    )(z, z, z, bias, lam_vecs, subln_g)


def _t5_buckets(n):
    rel = np.arange(n)
    max_exact = N_BUCKETS // 2
    nf = np.maximum(rel, 1).astype(np.float32)
    large = max_exact + (np.log(nf / np.float32(max_exact)) / np.float32(math.log(MAX_DIST / max_exact))
                         * np.float32(N_BUCKETS - max_exact)).astype(np.int32)
    large = np.minimum(large, N_BUCKETS - 1)
    return np.where(rel < max_exact, rel, large)


def _bias_kernel(rb_ref, idx_ref, o_ref, *, far_bucket):
    h = pl.program_id(0)
    far = rb_ref[far_bucket, h]
    for which in range(2):
        idx = idx_ref[which]
        acc = jnp.full(idx.shape, NEG_BIG, F32)
        for b in range(N_BUCKETS):
            acc = jnp.where(idx == b, (rb_ref[b, h] - far) * LOG2E, acc)
        o_ref[which] = acc


def _bias_tiles(rel_bias, seq, t):
    buckets = _t5_buckets(seq)
    assert np.all(buckets[t + 1:] == buckets[-1])
    jj, ii = np.meshgrid(np.arange(t), np.arange(t), indexing="ij")
    rel_d = ii - jj
    idx_d = np.where(rel_d >= 0, buckets[np.maximum(rel_d, 0)], N_BUCKETS)
    idx_s = buckets[rel_d + t]
    idx = jnp.asarray(np.stack([idx_d, idx_s]).astype(np.int32))
    kern = functools.partial(_bias_kernel, far_bucket=int(buckets[-1]))
    return pl.pallas_call(
        kern,
        grid=(N_HEADS,),
        in_specs=[pl.BlockSpec(memory_space=pltpu.SMEM),
                  pl.BlockSpec((2, t, t), lambda h: (0, 0, 0))],
        out_specs=pl.BlockSpec((None, 2, t, t), lambda h: (h, 0, 0, 0)),
        out_shape=jax.ShapeDtypeStruct((N_HEADS, 2, t, t), F32),
        compiler_params=_cparams(("arbitrary",)),
        name="bias_tiles",
    )(rel_bias.astype(F32), idx)


def _pool(p_ref, ph_ref, w_ref, sc_ref, buf_ref, first, pos, tm):
    buf_ref[POOL_HALO:POOL_HALO + tm, :] = p_ref[...].astype(F32)
    buf_ref[0:POOL_HALO, :] = jnp.where(first, 0.0, ph_ref[...].astype(F32))
    t_idx = pos * tm + lax.broadcasted_iota(jnp.int32, (tm, POOL_GW), 0)
    out = []
    for g, win in enumerate(POOL_WINDOWS):
        cols = slice(g * POOL_GW, (g + 1) * POOL_GW)
        ext = buf_ref[:, cols]
        cur = ext[POOL_HALO:, :]
        tot, span = ext, 1
        while span < win:
            tot = tot + pltpu.roll(tot, span, 0)
            span *= 2
        tot = tot[POOL_HALO:, :]
        cnt = jnp.minimum(t_idx + 1, win).astype(F32)
        d = (tot / cnt - cur).astype(BF16)
        y = jnp.dot(d, w_ref[g], preferred_element_type=F32)
        out.append((y * sc_ref[:, cols]).astype(BF16))
    return jnp.concatenate(out, axis=1)


def _merge_kernel(bc_ref, ba_ref, p_ref, ph_ref, gt_ref, x_ref, wc_ref, wa_ref, wp_ref, wo_ref,
                  pw_ref, psc_ref, o_ref, buf_ref, *, tm, tiles_per_seq):
    pos = pl.program_id(0) % tiles_per_seq

    def gate(b):
        return jax.nn.sigmoid(gt_ref[:, b * D_MODEL:(b + 1) * D_MODEL].astype(F32))

    m = gate(0) * jnp.dot(bc_ref[...], wc_ref[...], preferred_element_type=F32)
    m = m + gate(1) * jnp.dot(ba_ref[...], wa_ref[...], preferred_element_type=F32)
    b_pool = _pool(p_ref, ph_ref, pw_ref, psc_ref, buf_ref, pos == 0, pos, tm)
    m = m + gate(2) * jnp.dot(b_pool, wp_ref[...], preferred_element_type=F32)
    o_ref[...] = x_ref[...] + jnp.dot(m.astype(BF16), wo_ref[...], preferred_element_type=F32)


def _merge(b_conv, b_attn, z, x2, wc, wa, wp, wo, pool_w, pool_scale, seq, tm):
    n = x2.shape[0]
    hb = tm // POOL_HALO
    pc = COL_POOL // D_MODEL
    kern = functools.partial(_merge_kernel, tm=tm, tiles_per_seq=seq // tm)
    tile = pl.BlockSpec((tm, D_MODEL), lambda i: (i, 0))
    wspec = pl.BlockSpec((D_MODEL, D_MODEL), lambda i: (0, 0))
    return pl.pallas_call(
        kern,
        grid=(n // tm,),
        in_specs=[tile, tile,
                  pl.BlockSpec((tm, D_MODEL), lambda i: (i, pc)),
                  pl.BlockSpec((POOL_HALO, D_MODEL), lambda i: (jnp.maximum(i * hb - 1, 0), pc)),
                  pl.BlockSpec((tm, 3 * D_MODEL), lambda i: (i, COL_GATE // (3 * D_MODEL))),
                  tile, wspec, wspec, wspec, wspec,
                  pl.BlockSpec((len(POOL_WINDOWS), POOL_GW, POOL_GW), lambda i: (0, 0, 0)),
                  pl.BlockSpec((1, D_MODEL), lambda i: (0, 0))],
        out_specs=tile,
        out_shape=jax.ShapeDtypeStruct((n, D_MODEL), F32),
        scratch_shapes=[pltpu.VMEM((tm + POOL_HALO, D_MODEL), F32)],
        compiler_params=_cparams(("arbitrary",)),
        name="merge",
    )(b_conv, b_attn, z, z, z, x2, wc, wa, wp, wo, pool_w, pool_scale)


def _ffn_kernel(h_ref, g_ref, wu_ref, dw_ref, wd_ref, fg_ref, o_ref,
                xn_ref, carry_ref, acc_ref, *, tm, tiles_per_seq, final_norm):
    first = (pl.program_id(0) % tiles_per_seq) == 0

    @pl.when(first)
    def _():
        carry_ref[...] = jnp.zeros(carry_ref.shape, F32)

    h = h_ref[...]
    xn_ref[...] = _rms(h, g_ref[...]).astype(BF16)
    acc_ref[...] = jnp.zeros(acc_ref.shape, F32)

    def cols_of(c):
        return (slice(c * FFN_CHUNK, (c + 1) * FFN_CHUNK),
                slice(FFN_W + c * FFN_CHUNK, FFN_W + (c + 1) * FFN_CHUNK))

    def up(c):
        return [jnp.dot(xn_ref[...], wu_ref[:, cols], preferred_element_type=F32)
                for cols in cols_of(c)]

    def conv3(u, cols):
        ext = jnp.concatenate([carry_ref[:, cols], u], axis=0)
        carry_ref[:, cols] = u[tm - SUBLANES:, :]
        y = dw_ref[2:3, cols] * u
        for back in range(1, FFN_K):
            shifted = pltpu.roll(ext, back, 0)[SUBLANES:, :]
            y = y + dw_ref[FFN_K - 1 - back:FFN_K - back, cols] * shifted
        return y

    n_chunks = FFN_W // FFN_CHUNK
    ahead = 3
    pending = [up(c) for c in range(ahead)]
    for c in range(n_chunks):
        if c + ahead < n_chunks:
            pending.append(up(c + ahead))
        a, g = (conv3(u, cols) for u, cols in zip(pending.pop(0), cols_of(c)))
        mid = (g * jax.nn.sigmoid(g) * a).astype(BF16)
        acc_ref[...] += jnp.dot(mid, wd_ref[c * FFN_CHUNK:(c + 1) * FFN_CHUNK, :],
                                preferred_element_type=F32)

    out = h + acc_ref[...]
    if final_norm:
        out = _rms(out, fg_ref[...])
    o_ref[...] = out


def _ffn(h2, g, w_up, w_dw, w_down, final_g, seq, tm, final_norm):
    n = h2.shape[0]
    kern = functools.partial(_ffn_kernel, tm=tm, tiles_per_seq=seq // tm, final_norm=final_norm)
    tile = pl.BlockSpec((tm, D_MODEL), lambda i: (i, 0))
    vec = pl.BlockSpec((1, D_MODEL), lambda i: (0, 0))
    return pl.pallas_call(
        kern,
        grid=(n // tm,),
        in_specs=[tile, vec,
                  pl.BlockSpec((D_MODEL, 2 * FFN_W), lambda i: (0, 0)),
                  pl.BlockSpec((FFN_K, 2 * FFN_W), lambda i: (0, 0)),
                  pl.BlockSpec((FFN_W, D_MODEL), lambda i: (0, 0)),
                  vec],
        out_specs=tile,
        out_shape=jax.ShapeDtypeStruct((n, D_MODEL), F32),
        scratch_shapes=[pltpu.VMEM((tm, D_MODEL), BF16),
                        pltpu.VMEM((SUBLANES, 2 * FFN_W), F32),
                        pltpu.VMEM((tm, D_MODEL), F32)],
        compiler_params=_cparams(("arbitrary",)),
        name="conv_ffn",
    )(h2, g, w_up, w_dw, w_down, final_g)


def _tiles(seq):
    return dict(
        inproj_m=min(1024, seq), inproj_n=3 * D_MODEL,
        conv_m=min(256, seq),
        attn_t=min(512, seq), attn_heads=4, merge_m=min(512, seq), ffn_m=min(256, seq),
    )


def kernel(x, rel_bias, norm1_g, w_in, conv_dw_w, conv_dw_b, conv_ln_g, conv_ln_b, lam_q1, lam_k1, lam_q2, lam_k2, subln_g, pool_w, pool_scale, w_br_conv, w_br_attn, w_br_pool, w_o, norm2_g, ffn_up, ffn_dw, ffn_down, final_g):
    batch, seq, d = x.shape
    depth = w_in.shape[0]
    assert d == D_MODEL and w_in.shape[2] == IN_COLS
    ts = _tiles(seq)
    assert seq % ts["attn_t"] == 0 and ts["attn_t"] >= LANES

    colscale = np.ones((1, IN_COLS), np.float32)
    colscale[:, COL_Q:COL_K] = HEAD_DIM ** -0.5 * LOG2E
    colscale = jnp.asarray(colscale)
    bias = _bias_tiles(rel_bias, seq, ts["attn_t"])
    row = lambda v: v.astype(F32).reshape(1, -1)

    x2 = x.reshape(batch * seq, d)
    for l in range(depth):
        lam_init = 0.8 - 0.6 * math.exp(-0.3 * l)
        z = _inproj(x2, row(norm1_g[l]), w_in[l].astype(BF16), colscale,
                    ts["inproj_m"], ts["inproj_n"])
        b_conv = _conv_branch(z, conv_dw_w[l].astype(F32), row(conv_dw_b[l]),
                              row(conv_ln_g[l]), row(conv_ln_b[l]), seq, ts["conv_m"])
        lam_vecs = jnp.stack([lam_q1[l], lam_k1[l], lam_q2[l], lam_k2[l]]).astype(F32)
        b_attn = _attention(z, bias, lam_vecs, row(subln_g[l]), batch, seq, ts["attn_t"],
                            ts["attn_heads"], lam_init)
        h2 = _merge(b_conv, b_attn, z, x2,
                    w_br_conv[l].astype(BF16), w_br_attn[l].astype(BF16),
                    w_br_pool[l].astype(BF16), w_o[l].astype(BF16),
                    pool_w[l].astype(BF16), row(pool_scale[l]), seq, ts["merge_m"])
        x2 = _ffn(h2, row(norm2_g[l]), ffn_up[l].astype(BF16), ffn_dw[l].astype(F32),
                  ffn_down[l].astype(BF16), row(final_g), seq, ts["ffn_m"],
                  final_norm=(l == depth - 1))
    return x2.reshape(batch, seq, d)
```

```python
import functools
import math

import jax
import jax.numpy as jnp
import numpy as np
from jax import lax
from jax.experimental import pallas as pl
from jax.experimental.pallas import tpu as pltpu

F32 = jnp.float32
BF16 = jnp.bfloat16

D_MODEL = 1024
N_HEADS = 8
HEAD_DIM = 64
V_DIM = 2 * HEAD_DIM
CONV_K = 31
POOL_WINDOWS = (2, 4, 8, 16)
POOL_GW = D_MODEL // len(POOL_WINDOWS)
FFN_W = 2816
FFN_K = 3
N_BUCKETS = 32
MAX_DIST = 128
EPS = 1e-6

COL_CONV = 0
COL_Q = 2 * D_MODEL
COL_K = COL_Q + D_MODEL
COL_V = COL_K + D_MODEL
COL_POOL = COL_V + D_MODEL
COL_GATE = COL_POOL + D_MODEL
IN_COLS = COL_GATE + 3 * D_MODEL

LANES = 128
SUBLANES = 8
VMEM_LIMIT = 56 * 1024 * 1024
NEG_BIG = -1e30
F32_HUGE = 3.0e38
SOFTMAX_SUM_MIN = 1e-30
SOFTMAX_SUM_MAX = 1e30
ATTN_FAR_GROUP = 4
LOG2E = 1.4426950408889634
CONV_HALO = 32
POOL_HALO = 16
FFN_CHUNK = 256
INPROJ_CHUNKS = 3

def _cparams(sem):
    return pltpu.CompilerParams(dimension_semantics=sem, vmem_limit_bytes=VMEM_LIMIT)


def _rms(x, g):
    ms = jnp.mean(x * x, axis=-1, keepdims=True)
    return x * lax.rsqrt(ms + EPS) * g


def _inproj_kernel(x_ref, g_ref, w_ref, cs_ref, o_ref, xn_ref):
    @pl.when(pl.program_id(1) == 0)
    def _():
        xn_ref[...] = _rms(x_ref[...], g_ref[...]).astype(BF16)

    width = w_ref.shape[1] // INPROJ_CHUNKS
    for c in range(INPROJ_CHUNKS):
        cols = slice(c * width, (c + 1) * width)
        acc = jnp.dot(xn_ref[...], w_ref[:, cols], preferred_element_type=F32)
        o_ref[:, cols] = (acc * cs_ref[:, cols]).astype(BF16)


def _inproj(x2, g, w, colscale, tm, tn):
    n = x2.shape[0]
    return pl.pallas_call(
        _inproj_kernel,
        grid=(n // tm, IN_COLS // tn),
        in_specs=[
            pl.BlockSpec((tm, D_MODEL), lambda i, j: (i, 0)),
            pl.BlockSpec((1, D_MODEL), lambda i, j: (0, 0)),
            pl.BlockSpec((D_MODEL, tn), lambda i, j: (0, j)),
            pl.BlockSpec((1, tn), lambda i, j: (0, j)),
        ],
        out_specs=pl.BlockSpec((tm, tn), lambda i, j: (i, j)),
        out_shape=jax.ShapeDtypeStruct((n, IN_COLS), BF16),
        scratch_shapes=[pltpu.VMEM((tm, D_MODEL), BF16)],
        compiler_params=_cparams(("arbitrary", "arbitrary")),
        name="inproj",
    )(x2, g, w, colscale)


def _conv_kernel(a_ref, g_ref, ah_ref, gh_ref, w_ref, b_ref, lg_ref, lb_ref, o_ref,
                 h_ref, y_ref, *, tm, tiles_per_seq, rows):
    first = (pl.program_id(0) % tiles_per_seq) == 0
    n = tm + CONV_HALO
    main = a_ref[...].astype(F32) * jax.nn.sigmoid(g_ref[...].astype(F32))
    halo = ah_ref[...].astype(F32) * jax.nn.sigmoid(gh_ref[...].astype(F32))
    h = jnp.concatenate([jnp.where(first, 0.0, halo), main], axis=0)
    h_ref[0] = h
    for b in range(1, SUBLANES):
        h_ref[b] = pltpu.roll(h, n - b, 0)

    lead = CONV_HALO - (CONV_K - 1)

    def chunk(r, carry):
        r0 = pl.multiple_of(r * rows, rows)
        for cg in range(D_MODEL // LANES):
            cols = slice(cg * LANES, (cg + 1) * LANES)
            acc = jnp.broadcast_to(b_ref[:, cols], (rows, LANES))
            for b in range(SUBLANES):
                hb = h_ref[b, pl.ds(r0, rows + CONV_HALO), cols]
                for off in range(b, CONV_HALO + 1, SUBLANES):
                    j = off - lead
                    if 0 <= j < CONV_K:
                        acc = acc + w_ref[j:j + 1, cols] * hb[off - b:off - b + rows]
            y_ref[pl.ds(r0, rows), cols] = acc
        return carry

    lax.fori_loop(0, tm // rows, chunk, 0)

    y = y_ref[...]
    mu = jnp.mean(y, axis=-1, keepdims=True)
    yc = y - mu
    var = jnp.mean(yc * yc, axis=-1, keepdims=True)
    yn = yc * lax.rsqrt(var + EPS) * lg_ref[...] + lb_ref[...]
    o_ref[...] = (yn * jax.nn.sigmoid(yn)).astype(BF16)


def _conv_branch(z, w, b, lg, lb, seq, tm):
    n = z.shape[0]
    hb = tm // CONV_HALO
    halo_idx = lambda i: jnp.maximum(i * hb - 1, 0)
    kern = functools.partial(_conv_kernel, tm=tm, tiles_per_seq=seq // tm, rows=32)
    vec = pl.BlockSpec((1, D_MODEL), lambda i: (0, 0))
    return pl.pallas_call(
        kern,
        grid=(n // tm,),
        in_specs=[
            pl.BlockSpec((tm, D_MODEL), lambda i: (i, 0)),
            pl.BlockSpec((tm, D_MODEL), lambda i: (i, 1)),
            pl.BlockSpec((CONV_HALO, D_MODEL), lambda i: (halo_idx(i), 0)),
            pl.BlockSpec((CONV_HALO, D_MODEL), lambda i: (halo_idx(i), 1)),
            pl.BlockSpec((CONV_K, D_MODEL), lambda i: (0, 0)),
            vec, vec, vec,
        ],
        out_specs=pl.BlockSpec((tm, D_MODEL), lambda i: (i, 0)),
        out_shape=jax.ShapeDtypeStruct((n, D_MODEL), BF16),
        scratch_shapes=[pltpu.VMEM((SUBLANES, tm + CONV_HALO, D_MODEL), F32),
                        pltpu.VMEM((tm, D_MODEL), F32)],
        compiler_params=_cparams(("arbitrary",)),
        name="conv_branch",
    )(z, z, z, z, w, b, lg, lb)


def _attn_kernel(q_ref, k_ref, v_ref, bias_ref, lam_ref, sg_ref, o_ref,
                 r_ref, l_ref, acc_ref, *, t, hps, lam_init):
    qi = pl.program_id(2)
    lane = lax.broadcasted_iota(jnp.int32, (t, V_DIM), 1)
    qmaps = []
    for hh in range(hps):
        q = q_ref[:, hh * V_DIM:(hh + 1) * V_DIM]
        zero = jnp.zeros_like(q)
        qmaps += [jnp.where(lane < HEAD_DIM, q, zero), jnp.where(lane >= HEAD_DIM, q, zero)]

    def rows(ki):
        return pl.ds(pl.multiple_of(ki * t, t), t)

    def logits(ki, hh, mi, which):
        k = k_ref[rows(ki), hh * V_DIM:(hh + 1) * V_DIM]
        s = lax.dot_general(k, qmaps[2 * hh + mi], (((1,), (1,)), ((), ())),
                            preferred_element_type=F32)
        return s if which is None else s + bias_ref[hh, which]

    def pv(ki, hh, p):
        v = v_ref[rows(ki), hh * V_DIM:(hh + 1) * V_DIM]
        return lax.dot_general(v, p.astype(BF16), (((0,), (0,)), ((), ())),
                               preferred_element_type=F32)

    def blocks(specs, init=False):
        sums = {}

        def finish(hh, mi, bi, s):
            u = 2 * hh + mi
            p = jnp.exp2(s)
            lsum = jnp.sum(p, axis=0, keepdims=True)
            part = pv(specs[bi][0], hh, p)
            sums[u] = (lsum, part) if u not in sums else (sums[u][0] + lsum, sums[u][1] + part)
            if bi == len(specs) - 1:
                if init:
                    l_ref[u] = sums[u][0]
                    acc_ref[u] = sums[u][1]
                else:
                    l_ref[u] += sums[u][0]
                    acc_ref[u] += sums[u][1]

        prev = None
        for hh in range(hps):
            for mi in range(2):
                for bi, (ki, which) in enumerate(specs):
                    s = logits(ki, hh, mi, which)
                    if prev is not None:
                        finish(*prev)
                    prev = (hh, mi, bi, s)
        finish(*prev)

    n_far = jnp.maximum(qi - 1, 0)

    @pl.when(qi == 0)
    def _():
        blocks([(qi, 0)], init=True)

    @pl.when(qi >= 1)
    def _():
        blocks([(qi, 0), (qi - 1, 1)], init=True)

    def far_group(j, carry):
        blocks([(ATTN_FAR_GROUP * j + g, None) for g in range(ATTN_FAR_GROUP)])
        return carry

    lax.fori_loop(0, n_far // ATTN_FAR_GROUP, far_group, 0)
    g = ATTN_FAR_GROUP // 2
    while g >= 1:
        @pl.when((n_far & g) != 0)
        def _(g=g):
            base = n_far - (n_far & (2 * g - 1))
            blocks([(base + i, None) for i in range(g)])
        g //= 2

    flag = jnp.zeros((1, t), F32)
    for u in range(2 * hps):
        l = l_ref[u]
        in_range = jnp.logical_and(l > SOFTMAX_SUM_MIN, l < SOFTMAX_SUM_MAX)
        flag = jnp.maximum(flag, jnp.where(in_range, 0.0, 1.0))
        amax = jnp.max(jnp.abs(acc_ref[u]), axis=0, keepdims=True)
        flag = jnp.maximum(flag, jnp.where(amax < F32_HUGE, 0.0, 1.0))
    overflowed = jnp.max(flag, axis=1, keepdims=True)[0, 0] > 0.0

    @pl.when(overflowed)
    def _():
        r_ref[...] = jnp.full(r_ref.shape, NEG_BIG, F32)
        l_ref[...] = jnp.zeros(l_ref.shape, F32)
        acc_ref[...] = jnp.zeros(acc_ref.shape, F32)

        def exact(ki, which):
            for hh in range(hps):
                for mi in range(2):
                    u = 2 * hh + mi
                    s = logits(ki, hh, mi, which)
                    m_old = r_ref[u]
                    m_new = jnp.maximum(m_old, jnp.max(s, axis=0, keepdims=True))
                    alpha = jnp.exp2(m_old - m_new)
                    p = jnp.exp2(s - m_new)
                    l_ref[u] = alpha * l_ref[u] + jnp.sum(p, axis=0, keepdims=True)
                    acc_ref[u] = alpha * acc_ref[u] + pv(ki, hh, p)
                    r_ref[u] = m_new

        exact(qi, 0)

        @pl.when(qi >= 1)
        def _():
            exact(qi - 1, 1)

        def far_exact(j, carry):
            exact(j, None)
            return carry

        lax.fori_loop(0, n_far, far_exact, 0)

    lam = (jnp.exp(jnp.sum(lam_ref[0:1, :] * lam_ref[1:2, :], axis=-1, keepdims=True))
           - jnp.exp(jnp.sum(lam_ref[2:3, :] * lam_ref[3:4, :], axis=-1, keepdims=True))
           + lam_init)
    for hh in range(hps):
        u = 2 * hh
        o_t = (acc_ref[u] * (1.0 / l_ref[u])
               - acc_ref[u + 1] * (lam / l_ref[u + 1]))
        o = o_t.T
        o_ref[:, hh * V_DIM:(hh + 1) * V_DIM] = (
            _rms(o, sg_ref[...]) * (1.0 - lam_init)).astype(BF16)


def _attention(z, bias, lam_vecs, subln_g, batch, seq, t, hps, lam_init):
    n = z.shape[0]
    nq = seq // t
    w = hps * V_DIM
    kern = functools.partial(_attn_kernel, t=t, hps=hps, lam_init=lam_init)
    return pl.pallas_call(
        kern,
        grid=(batch, N_HEADS // hps, nq),
        in_specs=[
            pl.BlockSpec((t, w), lambda b, h, i: (b * nq + i, COL_Q // w + h)),
            pl.BlockSpec((seq, w), lambda b, h, i: (b, COL_K // w + h)),
            pl.BlockSpec((seq, w), lambda b, h, i: (b, COL_V // w + h)),
            pl.BlockSpec((hps, 2, t, t), lambda b, h, i: (h, 0, 0, 0),
                         pipeline_mode=pl.Buffered(1)),
            pl.BlockSpec((4, HEAD_DIM), lambda b, h, i: (0, 0)),
            pl.BlockSpec((1, V_DIM), lambda b, h, i: (0, 0)),
        ],
        out_specs=pl.BlockSpec((t, w), lambda b, h, i: (b * nq + i, h)),
        out_shape=jax.ShapeDtypeStruct((n, D_MODEL), BF16),
        scratch_shapes=[pltpu.VMEM((2 * hps, 1, t), F32),
                        pltpu.VMEM((2 * hps, 1, t), F32),
                        pltpu.VMEM((2 * hps, V_DIM, t), F32)],
        compiler_params=_cparams(("arbitrary", "arbitrary", "arbitrary")),
        name="diff_attn",
    )(z, z, z, bias, lam_vecs, subln_g)


def _t5_buckets(n):
    rel = np.arange(n)
    max_exact = N_BUCKETS // 2
    nf = np.maximum(rel, 1).astype(np.float32)
    large = max_exact + (np.log(nf / np.float32(max_exact)) / np.float32(math.log(MAX_DIST / max_exact))
                         * np.float32(N_BUCKETS - max_exact)).astype(np.int32)
    large = np.minimum(large, N_BUCKETS - 1)
    return np.where(rel < max_exact, rel, large)


def _bias_kernel(rb_ref, idx_ref, o_ref, *, far_bucket):
    h = pl.program_id(0)
    far = rb_ref[far_bucket, h]
    for which in range(2):
        idx = idx_ref[which]
        acc = jnp.full(idx.shape, NEG_BIG, F32)
        for b in range(N_BUCKETS):
            acc = jnp.where(idx == b, (rb_ref[b, h] - far) * LOG2E, acc)
        o_ref[which] = acc


def _bias_tiles(rel_bias, seq, t):
    buckets = _t5_buckets(seq)
    assert np.all(buckets[t + 1:] == buckets[-1])
    jj, ii = np.meshgrid(np.arange(t), np.arange(t), indexing="ij")
    rel_d = ii - jj
    idx_d = np.where(rel_d >= 0, buckets[np.maximum(rel_d, 0)], N_BUCKETS)
    idx_s = buckets[rel_d + t]
    idx = jnp.asarray(np.stack([idx_d, idx_s]).astype(np.int32))
    kern = functools.partial(_bias_kernel, far_bucket=int(buckets[-1]))
    return pl.pallas_call(
        kern,
        grid=(N_HEADS,),
        in_specs=[pl.BlockSpec(memory_space=pltpu.SMEM),
                  pl.BlockSpec((2, t, t), lambda h: (0, 0, 0))],
        out_specs=pl.BlockSpec((None, 2, t, t), lambda h: (h, 0, 0, 0)),
        out_shape=jax.ShapeDtypeStruct((N_HEADS, 2, t, t), F32),
        compiler_params=_cparams(("arbitrary",)),
        name="bias_tiles",
    )(rel_bias.astype(F32), idx)


def _pool_diff(p_ref, ph_ref, buf_ref, first, pos, tm):
    buf_ref[POOL_HALO:POOL_HALO + tm, :] = p_ref[...].astype(F32)
    buf_ref[0:POOL_HALO, :] = jnp.where(first, 0.0, ph_ref[...].astype(F32))
    t_idx = pos * tm + lax.broadcasted_iota(jnp.int32, (tm, POOL_GW), 0)
    out = []
    for g, win in enumerate(POOL_WINDOWS):
        cols = slice(g * POOL_GW, (g + 1) * POOL_GW)
        ext = buf_ref[:, cols]
        cur = ext[POOL_HALO:, :]
        tot, span = ext, 1
        while span < win:
            tot = tot + pltpu.roll(tot, span, 0)
            span *= 2
        tot = tot[POOL_HALO:, :]
        cnt = jnp.minimum(t_idx + 1, win).astype(F32)
        out.append((tot / cnt - cur).astype(BF16))
    return out


def _merge_kernel(bc_ref, ba_ref, p_ref, ph_ref, gt_ref, x_ref, wc_ref, wa_ref, wp_ref, wo_ref,
                  pw_ref, psc_ref, o_ref, buf_ref, *, tm, tiles_per_seq):
    pos = pl.program_id(0) % tiles_per_seq

    def gate(b):
        return jax.nn.sigmoid(gt_ref[:, b * D_MODEL:(b + 1) * D_MODEL].astype(F32))

    d = _pool_diff(p_ref, ph_ref, buf_ref, pos == 0, pos, tm)
    m = gate(0) * jnp.dot(bc_ref[...], wc_ref[...], preferred_element_type=F32)
    m = m + gate(1) * jnp.dot(ba_ref[...], wa_ref[...], preferred_element_type=F32)
    b_pool = jnp.concatenate(
        [(jnp.dot(d[g], pw_ref[g], preferred_element_type=F32)
          * psc_ref[:, g * POOL_GW:(g + 1) * POOL_GW]).astype(BF16)
         for g in range(len(POOL_WINDOWS))], axis=1)
    m = m + gate(2) * jnp.dot(b_pool, wp_ref[...], preferred_element_type=F32)
    o_ref[...] = x_ref[...] + jnp.dot(m.astype(BF16), wo_ref[...], preferred_element_type=F32)


def _merge(b_conv, b_attn, z, x2, wc, wa, wp, wo, pool_w, pool_scale, seq, tm):
    n = x2.shape[0]
    hb = tm // POOL_HALO
    pc = COL_POOL // D_MODEL
    kern = functools.partial(_merge_kernel, tm=tm, tiles_per_seq=seq // tm)
    tile = pl.BlockSpec((tm, D_MODEL), lambda i: (i, 0))
    wspec = pl.BlockSpec((D_MODEL, D_MODEL), lambda i: (0, 0))
    return pl.pallas_call(
        kern,
        grid=(n // tm,),
        in_specs=[tile, tile,
                  pl.BlockSpec((tm, D_MODEL), lambda i: (i, pc)),
                  pl.BlockSpec((POOL_HALO, D_MODEL), lambda i: (jnp.maximum(i * hb - 1, 0), pc)),
                  pl.BlockSpec((tm, 3 * D_MODEL), lambda i: (i, COL_GATE // (3 * D_MODEL))),
                  tile, wspec, wspec, wspec, wspec,
                  pl.BlockSpec((len(POOL_WINDOWS), POOL_GW, POOL_GW), lambda i: (0, 0, 0)),
                  pl.BlockSpec((1, D_MODEL), lambda i: (0, 0))],
        out_specs=tile,
        out_shape=jax.ShapeDtypeStruct((n, D_MODEL), F32),
        scratch_shapes=[pltpu.VMEM((tm + POOL_HALO, D_MODEL), F32)],
        compiler_params=_cparams(("arbitrary",)),
        name="merge",
    )(b_conv, b_attn, z, z, z, x2, wc, wa, wp, wo, pool_w, pool_scale)


def _ffn_kernel(h_ref, g_ref, wu_ref, dw_ref, wd_ref, fg_ref, o_ref,
                xn_ref, carry_ref, acc_ref, *, tm, tiles_per_seq, final_norm):
    first = (pl.program_id(0) % tiles_per_seq) == 0

    @pl.when(first)
    def _():
        carry_ref[...] = jnp.zeros(carry_ref.shape, F32)

    h = h_ref[...]
    xn_ref[...] = _rms(h, g_ref[...]).astype(BF16)
    acc_ref[...] = jnp.zeros(acc_ref.shape, F32)

    def cols_of(c):
        return (slice(c * FFN_CHUNK, (c + 1) * FFN_CHUNK),
                slice(FFN_W + c * FFN_CHUNK, FFN_W + (c + 1) * FFN_CHUNK))

    def up(c):
        return [jnp.dot(xn_ref[...], wu_ref[:, cols], preferred_element_type=F32)
                for cols in cols_of(c)]

    def conv3(u, cols):
        ext = jnp.concatenate([carry_ref[:, cols], u], axis=0)
        carry_ref[:, cols] = u[tm - SUBLANES:, :]
        y = dw_ref[2:3, cols] * u
        for back in range(1, FFN_K):
            shifted = pltpu.roll(ext, back, 0)[SUBLANES:, :]
            y = y + dw_ref[FFN_K - 1 - back:FFN_K - back, cols] * shifted
        return y

    n_chunks = FFN_W // FFN_CHUNK
    ahead = 3
    pending = [up(c) for c in range(ahead)]
    for c in range(n_chunks):
        if c + ahead < n_chunks:
            pending.append(up(c + ahead))
        a, g = (conv3(u, cols) for u, cols in zip(pending.pop(0), cols_of(c)))
        mid = (g * jax.nn.sigmoid(g) * a).astype(BF16)
        acc_ref[...] += jnp.dot(mid, wd_ref[c * FFN_CHUNK:(c + 1) * FFN_CHUNK, :],
                                preferred_element_type=F32)

    out = h + acc_ref[...]
    if final_norm:
        out = _rms(out, fg_ref[...])
    o_ref[...] = out


def _ffn(h2, g, w_up, w_dw, w_down, final_g, seq, tm, final_norm):
    n = h2.shape[0]
    kern = functools.partial(_ffn_kernel, tm=tm, tiles_per_seq=seq // tm, final_norm=final_norm)
    tile = pl.BlockSpec((tm, D_MODEL), lambda i: (i, 0))
    vec = pl.BlockSpec((1, D_MODEL), lambda i: (0, 0))
    return pl.pallas_call(
        kern,
        grid=(n // tm,),
        in_specs=[tile, vec,
                  pl.BlockSpec((D_MODEL, 2 * FFN_W), lambda i: (0, 0)),
                  pl.BlockSpec((FFN_K, 2 * FFN_W), lambda i: (0, 0)),
                  pl.BlockSpec((FFN_W, D_MODEL), lambda i: (0, 0)),
                  vec],
        out_specs=tile,
        out_shape=jax.ShapeDtypeStruct((n, D_MODEL), F32),
        scratch_shapes=[pltpu.VMEM((tm, D_MODEL), BF16),
                        pltpu.VMEM((SUBLANES, 2 * FFN_W), F32),
                        pltpu.VMEM((tm, D_MODEL), F32)],
        compiler_params=_cparams(("arbitrary",)),
        name="conv_ffn",
    )(h2, g, w_up, w_dw, w_down, final_g)


def _tiles(seq):
    return dict(
        inproj_m=min(1024, seq), inproj_n=IN_COLS // 2,
        conv_m=min(512, seq),
        attn_t=min(512, seq), attn_heads=4, merge_m=min(512, seq), ffn_m=min(256, seq),
    )


def kernel(x, rel_bias, norm1_g, w_in, conv_dw_w, conv_dw_b, conv_ln_g, conv_ln_b, lam_q1, lam_k1, lam_q2, lam_k2, subln_g, pool_w, pool_scale, w_br_conv, w_br_attn, w_br_pool, w_o, norm2_g, ffn_up, ffn_dw, ffn_down, final_g):
    batch, seq, d = x.shape
    depth = w_in.shape[0]
    assert d == D_MODEL and w_in.shape[2] == IN_COLS
    ts = _tiles(seq)
    assert seq % ts["attn_t"] == 0 and ts["attn_t"] >= LANES

    colscale = np.ones((1, IN_COLS), np.float32)
    colscale[:, COL_Q:COL_K] = HEAD_DIM ** -0.5 * LOG2E
    colscale = jnp.asarray(colscale)
    bias = _bias_tiles(rel_bias, seq, ts["attn_t"])
    row = lambda v: v.astype(F32).reshape(1, -1)

    x2 = x.reshape(batch * seq, d)
    for l in range(depth):
        lam_init = 0.8 - 0.6 * math.exp(-0.3 * l)
        z = _inproj(x2, row(norm1_g[l]), w_in[l].astype(BF16), colscale,
                    ts["inproj_m"], ts["inproj_n"])
        b_conv = _conv_branch(z, conv_dw_w[l].astype(F32), row(conv_dw_b[l]),
                              row(conv_ln_g[l]), row(conv_ln_b[l]), seq, ts["conv_m"])
        lam_vecs = jnp.stack([lam_q1[l], lam_k1[l], lam_q2[l], lam_k2[l]]).astype(F32)
        b_attn = _attention(z, bias, lam_vecs, row(subln_g[l]), batch, seq, ts["attn_t"],
                            ts["attn_heads"], lam_init)
        h2 = _merge(b_conv, b_attn, z, x2,
                    w_br_conv[l].astype(BF16), w_br_attn[l].astype(BF16),
                    w_br_pool[l].astype(BF16), w_o[l].astype(BF16),
                    pool_w[l].astype(BF16), row(pool_scale[l]), seq, ts["merge_m"])
        x2 = _ffn(h2, row(norm2_g[l]), ffn_up[l].astype(BF16), ffn_dw[l].astype(F32),
                  ffn_down[l].astype(BF16), row(final_g), seq, ts["ffn_m"],
                  final_norm=(l == depth - 1))
    return x2.reshape(batch, seq, d)
```
